```python
import math
import jax
import jax.numpy as jnp
from jax import lax
import numpy as np

D_MODEL = 4096
BATCH = 2
SEQ = 8192
DEPTH = 2

CTX_LEN = 256
GRID_W = 64
EPS = 1e-6
GATE_FLOOR = 1e-30
HEAD_DIM = 128
DN_HEADS = 8
DN_WIDTH = DN_HEADS * HEAD_DIM
DN_CONV = 5
HG_HEADS = 8
HG_WIDTH = HG_HEADS * HEAD_DIM
ATT_Q_HEADS = 16
ATT_KV_HEADS = 4
ATT_Q_WIDTH = ATT_Q_HEADS * HEAD_DIM
ATT_KV_WIDTH = ATT_KV_HEADS * HEAD_DIM
ATT_BLOCK = 128
ROPE_THETA = 10000.0
CHUNK = 64
MIX_WIDTH = DN_WIDTH + HG_WIDTH + ATT_Q_WIDTH
DN_COLS = 4 * DN_WIDTH + 4 * DN_HEADS
HG_COLS = 5 * HG_WIDTH
ATT_COLS = ATT_Q_WIDTH + 2 * ATT_KV_WIDTH
IN_WIDTH = DN_COLS + HG_COLS + ATT_COLS
N_EXPERTS = 64
TOP_K = 8
EXPERT_FF = 256
SHARED_FF = 256
N_GROUPS = 8
TOPK_GROUPS = 4
ROUTED_SCALE = 2.5
MOE_BLOCK = 256
N_MOD = 6
F32 = jnp.float32

kernel_name = 'hybrid_gdn_hgrn2_gqa_moe_dit'


def rmsnorm(x, g):
    xf = x.astype(F32)
    y = xf * lax.rsqrt(jnp.mean(xf * xf, axis=-1, keepdims=True) + EPS)
    return (y * g.astype(F32)).astype(x.dtype)


def l2norm(x):
    return x * lax.rsqrt(jnp.sum(x * x, axis=-1, keepdims=True) + EPS)


def modulate(h, shift, scale):
    return h * (1.0 + scale) + shift


def to_heads(t, n_heads):
    b, n, _ = t.shape
    return t.reshape(b, n, n_heads, -1).transpose(0, 2, 1, 3)


def from_heads(t):
    b, h, n, d = t.shape
    return t.transpose(0, 2, 1, 3).reshape(b, n, h * d)


def short_conv(x, w):
    width = w.shape[0]
    return lax.conv_general_dilated(x, w[:, None, :].astype(x.dtype), window_strides=(1,),
                                    padding=[(width // 2, width // 2)],
                                    dimension_numbers=('NWC', 'WIO', 'NWC'),
                                    feature_group_count=x.shape[-1])


def gated_delta_chunked(q, k, v, g, beta, s0):
    b, h, n_tok, _ = q.shape
    n = n_tok // CHUNK
    q, k, v = (t.reshape(b, h, n, CHUNK, t.shape[-1]) for t in (q, k, v))
    g = g.reshape(b, h, n, CHUNK)
    beta = beta.reshape(b, h, n, CHUNK)
    gc = jnp.cumsum(g, axis=-1)
    tri = jnp.tril(jnp.ones((CHUNK, CHUNK), bool))
    strict = jnp.tril(jnp.ones((CHUNK, CHUNK), bool), -1)
    decay_ts = jnp.exp(jnp.where(tri, gc[..., :, None] - gc[..., None, :], -jnp.inf))
    kb = k * beta[..., None]
    a = jnp.where(strict, jnp.einsum('bhntd,bhnsd->bhnts', kb, k) * decay_ts, 0.0)
    eye = jnp.eye(CHUNK, dtype=q.dtype)
    t_mat = lax.linalg.triangular_solve(eye + a, jnp.broadcast_to(eye, a.shape), left_side=True,
                                        lower=True, unit_diagonal=True)
    u = t_mat @ (v * beta[..., None])
    w = t_mat @ (kb * jnp.exp(gc)[..., None])
    qk = jnp.einsum('bhntd,bhnsd->bhnts', q, k) * decay_ts
    q_dec = q * jnp.exp(gc)[..., None]
    k_dec = k * jnp.exp(gc[..., -1:] - gc)[..., None]
    last = jnp.exp(gc[..., -1])

    def step(s, inp):
        qk_c, qd_c, kd_c, u_c, w_c, last_c = inp
        v_new = u_c - w_c @ s
        o = qd_c @ s + qk_c @ v_new
        s = s * last_c[..., None, None] + jnp.swapaxes(kd_c, -1, -2) @ v_new
        return s, o

    xs = tuple(jnp.moveaxis(t, 2, 0) for t in (qk, q_dec, k_dec, u, w, last))
    s_fin, o = lax.scan(step, s0, xs)
    return jnp.moveaxis(o, 0, 2).reshape(b, h, n_tok, -1), s_fin


def gla_chunked(q, k, v, log_f, s0):
    b, h, n_tok, _ = q.shape
    n = n_tok // CHUNK
    resh = lambda t: jnp.moveaxis(t.reshape(b, h, n, CHUNK, t.shape[-1]), 2, 0)
    tri = jnp.tril(jnp.ones((CHUNK, CHUNK), bool))[:, :, None]

    def step(s, inp):
        q_c, k_c, v_c, lf_c = inp
        cum = jnp.cumsum(lf_c, axis=2)
        pair = jnp.exp(jnp.where(tri, cum[:, :, :, None, :] - cum[:, :, None, :, :], -jnp.inf))
        att = jnp.einsum('bhtd,bhsd,bhtsd->bhts', q_c, k_c, pair)
        o = (q_c * jnp.exp(cum)) @ s + att @ v_c
        cum_last = cum[:, :, -1:, :]
        s = jnp.exp(cum_last[:, :, 0, :])[..., None] * s + jnp.swapaxes(k_c * jnp.exp(cum_last - cum), -1, -2) @ v_c
        return s, o

    s_fin, o = lax.scan(step, s0, tuple(resh(t) for t in (q, k, v, log_f)))
    return jnp.moveaxis(o, 0, 2).reshape(b, h, n_tok, -1), s_fin


def ctx_then_latent(scan_fn, ctx_args, lat_args, s0, reverse):
    flip = (lambda t: jnp.flip(t, axis=2)) if reverse else (lambda t: t)
    o_c, s_c = scan_fn(*(flip(t) for t in ctx_args), s0)
    o_l, _ = scan_fn(*(flip(t) for t in lat_args), s_c)
    return flip(o_c), flip(o_l)


def deltanet_mixer(p_c, p_l, conv_w, a_log, dt_bias, norm_g, ctx_out):
    out_dtype = p_l.dtype

    def prep(p):
        bsz, n, _ = p.shape
        p = p.astype(F32)
        qkv, z, ab = jnp.split(p, [3 * DN_WIDTH, 4 * DN_WIDTH], axis=-1)
        qkv = jax.nn.silu(short_conv(qkv, conv_w.astype(F32)))
        q, k, v = (to_heads(t, DN_HEADS) for t in jnp.split(qkv, 3, axis=-1))
        q = l2norm(q) * HEAD_DIM ** -0.5
        k = l2norm(k)
        ab = ab.reshape(bsz, n, 2, 2, DN_HEADS).transpose(2, 3, 0, 4, 1)
        log_decay = -jnp.exp(a_log.astype(F32))[:, None, :, None] * jax.nn.softplus(
            ab[0] + dt_bias.astype(F32)[:, None, :, None])
        beta = jax.nn.sigmoid(ab[1])
        return [(q, k, v, log_decay[d], beta[d]) for d in range(2)], z

    dirs_c, z_c = prep(p_c)
    dirs_l, z_l = prep(p_l)
    s0 = jnp.zeros((p_l.shape[0], DN_HEADS, HEAD_DIM, HEAD_DIM), F32)
    fwd = ctx_then_latent(gated_delta_chunked, dirs_c[0], dirs_l[0], s0, reverse=False)
    bwd = ctx_then_latent(gated_delta_chunked, dirs_c[1], dirs_l[1], s0, reverse=True)
    finish = lambda o, z: (from_heads(rmsnorm(o, norm_g)) * jax.nn.silu(z)).astype(out_dtype)
    o_c = finish(fwd[0] + bwd[0], z_c) if ctx_out else None
    return o_c, finish(fwd[1] + bwd[1], z_l)


def hgrn2_mixer(p_c, p_l, lower_bound, norm_g, ctx_out):
    out_dtype = p_l.dtype
    lb = lower_bound.astype(F32)

    def prep(p):
        bsz, n, _ = p.shape
        p = p.astype(F32)
        q, f, i, g = jnp.split(p, [HG_WIDTH, 3 * HG_WIDTH, 4 * HG_WIDTH], axis=-1)
        q = to_heads(jax.nn.silu(q), HG_HEADS)
        i = to_heads(i, HG_HEADS)
        f = f.reshape(bsz, n, 2, HG_WIDTH)
        f_gate = lb + (1.0 - lb) * jax.nn.sigmoid(f)
        log_f = jnp.log(jnp.maximum(f_gate, GATE_FLOOR))
        key = (1.0 - lb) * jax.nn.sigmoid(-f)
        dirs = []
        for d in range(2):
            dirs.append((q, to_heads(key[:, :, d], HG_HEADS), i, to_heads(log_f[:, :, d], HG_HEADS)))
        return dirs, g

    dirs_c, g_c = prep(p_c)
    dirs_l, g_l = prep(p_l)
    s0 = jnp.zeros((p_l.shape[0], HG_HEADS, HEAD_DIM, HEAD_DIM), F32)
    fwd = ctx_then_latent(gla_chunked, dirs_c[0], dirs_l[0], s0, reverse=False)
    bwd = ctx_then_latent(gla_chunked, dirs_c[1], dirs_l[1], s0, reverse=True)
    finish = lambda o, g: (from_heads(rmsnorm(o, norm_g)) * jax.nn.silu(g)).astype(out_dtype)
    o_c = finish(fwd[0] + bwd[0], g_c) if ctx_out else None
    return o_c, finish(fwd[1] + bwd[1], g_l)


def axial_rope_tables(n_tokens):
    rows = n_tokens // GRID_W
    row, col = jnp.meshgrid(jnp.arange(rows), jnp.arange(GRID_W), indexing='ij')
    half = HEAD_DIM // 2
    inv_freq = ROPE_THETA ** (-jnp.arange(0, half, 2, dtype=F32) / half)
    ang_r = row.reshape(-1, 1).astype(F32) * inv_freq
    ang_c = col.reshape(-1, 1).astype(F32) * inv_freq
    ang = jnp.concatenate([ang_r, ang_r, ang_c, ang_c], axis=-1)
    return jnp.cos(ang), jnp.sin(ang)


def rotate_half(t):
    t1, t2 = jnp.split(t, 2, axis=-1)
    return jnp.concatenate([-t2, t1], axis=-1)


def apply_axial_rope(x, cos, sin):
    x_row, x_col = jnp.split(x, 2, axis=-1)
    rot = jnp.concatenate([rotate_half(x_row), rotate_half(x_col)], axis=-1)
    return (x * cos[None, :, None, :] + rot * sin[None, :, None, :]).astype(x.dtype)


def attend(q, k, v):
    bsz, lq, hq, d = q.shape
    hkv = k.shape[2]
    qg = q.reshape(bsz, lq, hkv, hq // hkv, d)
    s = jnp.einsum('bqhgd,bkhd->bhgqk', qg, k).astype(F32) * d ** -0.5
    pr = jax.nn.softmax(s, axis=-1).astype(v.dtype)
    return jnp.einsum('bhgqk,bkhd->bqhgd', pr, v).reshape(bsz, lq, hq * d)


def gqa_mixer(p_c, p_l, q_norm_g, k_norm_g, cos, sin, ctx_out):
    def prep(p):
        bsz, n, _ = p.shape
        q, k, v = jnp.split(p, [ATT_Q_WIDTH, ATT_Q_WIDTH + ATT_KV_WIDTH], axis=-1)
        return (rmsnorm(q.reshape(bsz, n, ATT_Q_HEADS, HEAD_DIM), q_norm_g),
                rmsnorm(k.reshape(bsz, n, ATT_KV_HEADS, HEAD_DIM), k_norm_g),
                v.reshape(bsz, n, ATT_KV_HEADS, HEAD_DIM))

    q_c, k_c, v_c = prep(p_c)
    q_l, k_l, v_l = prep(p_l)
    q_l = apply_axial_rope(q_l, cos, sin)
    k_l = apply_axial_rope(k_l, cos, sin)
    k_all = jnp.concatenate([k_c, k_l], axis=1)
    v_all = jnp.concatenate([v_c, v_l], axis=1)
    bsz, n = q_l.shape[:2]
    nb = n // ATT_BLOCK
    q_blocks = jnp.swapaxes(q_l.reshape(bsz, nb, ATT_BLOCK, ATT_Q_HEADS, HEAD_DIM), 0, 1)
    o_l = lax.map(lambda qb: attend(qb, k_all, v_all), q_blocks)
    o_l = jnp.swapaxes(o_l, 0, 1).reshape(bsz, n, ATT_Q_WIDTH)
    o_c = attend(q_c, k_c, v_c) if ctx_out else None
    return o_c, o_l


def swiglu(h, w_gate, w_up, w_down):
    return (jax.nn.silu(h @ w_gate) * (h @ w_up)) @ w_down


def routed_experts(h, expert_idx, expert_w, w_gate, w_up, w_down):
    n_tok, d = h.shape
    n_assign = n_tok * TOP_K
    flat_e = expert_idx.reshape(-1)
    order = jnp.argsort(flat_e)
    e_sorted = flat_e[order]
    counts = jnp.bincount(flat_e, length=N_EXPERTS)
    padded = (counts + MOE_BLOCK - 1) // MOE_BLOCK * MOE_BLOCK
    pad_end = jnp.cumsum(padded)
    pad_start = pad_end - padded
    grp_start = jnp.cumsum(counts) - counts
    dest = pad_start[e_sorted] + jnp.arange(n_assign) - grp_start[e_sorted]
    n_blocks = -(-n_assign // MOE_BLOCK) + N_EXPERTS
    n_rows = n_blocks * MOE_BLOCK
    row_tok = jnp.full((n_rows,), n_tok, jnp.int32).at[dest].set((order // TOP_K).astype(jnp.int32))
    row_w = jnp.zeros((n_rows,), F32).at[dest].set(expert_w.reshape(-1).astype(F32)[order])
    block_e = jnp.minimum(jnp.searchsorted(pad_end, jnp.arange(n_blocks) * MOE_BLOCK, side='right'),
                          N_EXPERTS - 1)
    h_pad = jnp.concatenate([h, jnp.zeros((1, d), h.dtype)], axis=0)

    def block_step(acc, blk):
        tok, wt, e = blk
        yb = swiglu(h_pad[tok], w_gate[e], w_up[e], w_down[e])
        return acc.at[tok].add(yb.astype(F32) * wt[:, None]), None

    acc, _ = lax.scan(block_step, jnp.zeros((n_tok + 1, d), F32),
                      (row_tok.reshape(n_blocks, MOE_BLOCK), row_w.reshape(n_blocks, MOE_BLOCK), block_e))
    return acc[:n_tok].astype(h.dtype)


def moe_ffn(h, w_router, router_bias, w_gate, w_up, w_down, ws_gate, ws_up, ws_down):
    n_tok = h.shape[0]
    scores = jax.nn.sigmoid((h @ w_router).astype(F32))
    sel = scores + router_bias.astype(F32)
    grp_score = lax.top_k(sel.reshape(n_tok, N_GROUPS, -1), 2)[0].sum(-1)
    _, grp_idx = lax.top_k(grp_score, TOPK_GROUPS)
    keep = jnp.any(grp_idx[:, :, None] == jnp.arange(N_GROUPS), axis=1)
    keep = jnp.repeat(keep, N_EXPERTS // N_GROUPS, axis=1)
    _, idx = lax.top_k(jnp.where(keep, sel, -jnp.inf), TOP_K)
    w = jnp.take_along_axis(scores, idx, axis=-1)
    w = w / jnp.sum(w, axis=-1, keepdims=True) * ROUTED_SCALE
    return routed_experts(h, idx, w, w_gate, w_up, w_down) + swiglu(h, ws_gate, ws_up, ws_down)


def setup_inputs(seed: int = 0) -> dict:
    key = jax.random.key(seed)
    ks = iter(jax.random.split(key, 40))
    nrm = lambda shape, scale: jax.random.normal(next(ks), shape, F32) * scale
    gain = lambda shape: 1.0 + 0.02 * jax.random.normal(next(ks), shape, F32)
    d = D_MODEL
    a_log = jnp.log(jax.random.uniform(next(ks), (DEPTH, 2, DN_HEADS), F32, minval=1.0, maxval=16.0))
    dt = jnp.exp(jax.random.uniform(next(ks), (DEPTH, 2, DN_HEADS), F32,
                                    minval=math.log(1e-3), maxval=math.log(1e-1)))
    dt_bias = dt + jnp.log(-jnp.expm1(-dt))
    return {
        'x': nrm((BATCH, SEQ, d), 1.0),
        'c': nrm((BATCH, d), 1.0),
        'ctx': nrm((BATCH, CTX_LEN, d), 1.0),
        'c_ctx': nrm((d,), 1.0),
        'w_mod': nrm((DEPTH, d, N_MOD * d), 0.5 * d ** -0.5),
        'b_mod': nrm((DEPTH, N_MOD * d), 0.01),
        'norm1_g': gain((DEPTH, d)),
        'norm2_g': gain((DEPTH, d)),
        'w_in': nrm((DEPTH, d, IN_WIDTH), d ** -0.5),
        'w_out': nrm((DEPTH, MIX_WIDTH, d), MIX_WIDTH ** -0.5),
        'dn_conv_w': nrm((DEPTH, DN_CONV, 3 * DN_WIDTH), DN_CONV ** -0.5),
        'dn_a_log': a_log,
        'dn_dt_bias': dt_bias,
        'dn_norm_g': gain((DEPTH, HEAD_DIM)),
        'hg_lb_logits': nrm((DEPTH, 2, HG_WIDTH), 0.1),
        'hg_norm_g': gain((DEPTH, HEAD_DIM)),
        'att_q_norm_g': gain((DEPTH, HEAD_DIM)),
        'att_k_norm_g': gain((DEPTH, HEAD_DIM)),
        'w_router': nrm((DEPTH, d, N_EXPERTS), d ** -0.5),
        'router_bias': nrm((DEPTH, N_EXPERTS), 0.01),
        'w_exp_gate': nrm((DEPTH, N_EXPERTS, d, EXPERT_FF), d ** -0.5),
        'w_exp_up': nrm((DEPTH, N_EXPERTS, d, EXPERT_FF), d ** -0.5),
        'w_exp_down': nrm((DEPTH, N_EXPERTS, EXPERT_FF, d), EXPERT_FF ** -0.5),
        'w_sh_gate': nrm((DEPTH, d, SHARED_FF), d ** -0.5),
        'w_sh_up': nrm((DEPTH, d, SHARED_FF), d ** -0.5),
        'w_sh_down': nrm((DEPTH, SHARED_FF, d), SHARED_FF ** -0.5),
        'final_norm_g': gain((d,)),
    }


def reference(x, c, ctx, c_ctx, w_mod, b_mod, norm1_g, norm2_g, w_in, w_out,
              dn_conv_w, dn_a_log, dn_dt_bias, dn_norm_g, hg_lb_logits, hg_norm_g,
              att_q_norm_g, att_k_norm_g, w_router, router_bias, w_exp_gate, w_exp_up,
              w_exp_down, w_sh_gate, w_sh_up, w_sh_down, final_norm_g):
    bsz, n_lat, d = x.shape
    n_ctx = ctx.shape[1]
    cos, sin = axial_rope_tables(n_lat)
    lb_p = jax.nn.softmax(hg_lb_logits.astype(F32), axis=0)
    lower_bounds = jnp.cumsum(lb_p, axis=0) - lb_p[0]
    xl, xc = x, ctx
    for l in range(DEPTH):
        update_ctx = l < DEPTH - 1
        mod_l = jax.nn.silu(c) @ w_mod[l] + b_mod[l]
        mod_c = jax.nn.silu(c_ctx) @ w_mod[l] + b_mod[l]
        sh1_l, sc1_l, g1_l, sh2_l, sc2_l, g2_l = jnp.split(mod_l[:, None, :], N_MOD, axis=-1)
        sh1_c, sc1_c, g1_c, sh2_c, sc2_c, g2_c = jnp.split(mod_c[None, None, :], N_MOD, axis=-1)
        p_l = modulate(rmsnorm(xl, norm1_g[l]), sh1_l, sc1_l) @ w_in[l]
        p_c = modulate(rmsnorm(xc, norm1_g[l]), sh1_c, sc1_c) @ w_in[l]
        pa_l, pb_l, pg_l = jnp.split(p_l, [DN_COLS, DN_COLS + HG_COLS], axis=-1)
        pa_c, pb_c, pg_c = jnp.split(p_c, [DN_COLS, DN_COLS + HG_COLS], axis=-1)
        oa_c, oa_l = deltanet_mixer(pa_c, pa_l, dn_conv_w[l], dn_a_log[l], dn_dt_bias[l], dn_norm_g[l], update_ctx)
        ob_c, ob_l = hgrn2_mixer(pb_c, pb_l, lower_bounds[l], hg_norm_g[l], update_ctx)
        og_c, og_l = gqa_mixer(pg_c, pg_l, att_q_norm_g[l], att_k_norm_g[l], cos, sin, update_ctx)
        xl = xl + g1_l * (jnp.concatenate([oa_l, ob_l, og_l], axis=-1) @ w_out[l])
        moe_w = (w_router[l], router_bias[l], w_exp_gate[l], w_exp_up[l], w_exp_down[l],
                 w_sh_gate[l], w_sh_up[l], w_sh_down[l])
        hl = modulate(rmsnorm(xl, norm2_g[l]), sh2_l, sc2_l).reshape(bsz * n_lat, d)
        if update_ctx:
            xc = xc + g1_c * (jnp.concatenate([oa_c, ob_c, og_c], axis=-1) @ w_out[l])
            hc = modulate(rmsnorm(xc, norm2_g[l]), sh2_c, sc2_c).reshape(bsz * n_ctx, d)
            y = moe_ffn(jnp.concatenate([hc, hl], axis=0), *moe_w)
            xc = xc + g2_c * y[:bsz * n_ctx].reshape(bsz, n_ctx, d)
            y_l = y[bsz * n_ctx:]
        else:
            y_l = moe_ffn(hl, *moe_w)
        xl = xl + g2_l * y_l.reshape(bsz, n_lat, d)
    return rmsnorm(xl, final_norm_g)
```

```python
import functools

import numpy as np
import jax
import jax.numpy as jnp
from jax import lax
from jax.experimental import pallas as pl
from jax.experimental.pallas import tpu as pltpu

F32 = jnp.float32
BF16 = jnp.bfloat16

EPS = 1e-6
GATE_FLOOR = 1e-30
HEAD_DIM = 128
DN_HEADS = 8
DN_WIDTH = DN_HEADS * HEAD_DIM
DN_CONV = 5
HG_HEADS = 8
HG_WIDTH = HG_HEADS * HEAD_DIM
ATT_Q_HEADS = 16
ATT_KV_HEADS = 4
ATT_GROUP = ATT_Q_HEADS // ATT_KV_HEADS
ATT_Q_WIDTH = ATT_Q_HEADS * HEAD_DIM
ATT_KV_WIDTH = ATT_KV_HEADS * HEAD_DIM
GRID_W = 64
ROPE_THETA = 10000.0
CHUNK = 64
N_EXPERTS = 64
TOP_K = 8
N_GROUPS = 8
TOPK_GROUPS = 4
GROUP_SIZE = N_EXPERTS // N_GROUPS
ROUTED_SCALE = 2.5
N_MOD = 6

C_AQ = 0
C_AK = C_AQ + ATT_Q_WIDTH
C_AV = C_AK + ATT_KV_WIDTH
C_DQ = C_AV + ATT_KV_WIDTH
C_DZ = C_DQ + 3 * DN_WIDTH
C_HQ = C_DZ + DN_WIDTH
C_HF = C_HQ + HG_WIDTH
C_HI = C_HF + 2 * HG_WIDTH
C_HG = C_HI + HG_WIDTH
C_AB = C_HG + HG_WIDTH
LANES = 128
SUBLANES = 8
P_WIDTH = C_AB + 4 * LANES

MOE_ROWS = 256
VMEM_LIMIT = 56 * 1024 * 1024


def _cparams(sem, vmem=VMEM_LIMIT):
    return pltpu.CompilerParams(dimension_semantics=sem, vmem_limit_bytes=vmem)


def _pick(n, cap, mult):
    if n <= cap:
        return n
    for t in range(cap - cap % mult, 0, -mult):
        if n % t == 0:
            return t
    raise ValueError(f"no tile for {n} (cap {cap}, multiple of {mult})")


def _silu(x):
    return x * (1.0 / (1.0 + jnp.exp(-x)))


def _sigmoid(x):
    return 1.0 / (1.0 + jnp.exp(-x))


def _dot(a, b):
    return jnp.dot(a, b, preferred_element_type=F32)


def _dot_nt(a, b):
    return lax.dot_general(a, b, (((1,), (1,)), ((), ())), preferred_element_type=F32)


def _dot_tn(a, b):
    return lax.dot_general(a, b, (((0,), (0,)), ((), ())), preferred_element_type=F32)


def _split2(a):
    hi = a.astype(BF16)
    lo = (a - hi.astype(F32)).astype(BF16)
    return hi, lo


def _dot3(a, b):
    ah, al = _split2(a)
    bh, bl = _split2(b)
    return _dot(ah, bh) + _dot(ah, bl) + _dot(al, bh)


_LEVELS = (32, 16, 8, 4, 2, 1)


@functools.lru_cache(maxsize=None)
def _dir_consts():
    c = CHUNK
    t = np.arange(c)
    incl = [(t[None, :] <= t[:, None]), (t[None, :] >= t[:, None])]
    cms, masks = [], []
    for d in range(2):
        md = incl[d].astype(np.float32)
        blocks, mks = [md], []
        for b in _LEVELS:
            blk = t // (2 * b)
            ref = blk * 2 * b + (b - 1 if d == 0 else b)
            blocks.append(md[ref, :])
            same = blk[:, None] == blk[None, :]
            late = (t % (2 * b)) >= b
            if d == 0:
                mks.append(same & late[:, None] & ~late[None, :])
            else:
                mks.append(same & ~late[:, None] & late[None, :])
        mks.append(np.eye(c, dtype=bool))
        blocks.append(np.ones((c, c), np.float32))
        cms.append(np.concatenate(blocks, 0))
        masks.append(np.stack(mks).astype(np.float32))
    cm = np.stack(cms)
    lv = np.stack(masks)
    incl_f = np.stack([m.astype(np.float32) for m in incl])
    bd16 = ((t[:, None] // 16) == (t[None, :] // 16)).astype(np.float32)
    return cm, lv, incl_f, bd16


def _mod_kernel(c_ref, w_ref, b_ref, o_ref):
    a = _silu(c_ref[...]).astype(BF16)
    o_ref[0] = _dot(a, w_ref[0].astype(BF16)) + b_ref[0]


def _mod_vectors(c_all, w_mod, b_mod):
    depth, d, n = w_mod.shape
    tn = _pick(n, 512, LANES)
    return pl.pallas_call(
        _mod_kernel,
        grid=(depth, n // tn),
        in_specs=[
            pl.BlockSpec((SUBLANES, d), lambda l, j: (0, 0)),
            pl.BlockSpec((1, d, tn), lambda l, j: (l, 0, j)),
            pl.BlockSpec((1, 1, tn), lambda l, j: (l, 0, j)),
        ],
        out_specs=pl.BlockSpec((1, SUBLANES, tn), lambda l, j: (l, 0, j)),
        out_shape=jax.ShapeDtypeStruct((depth, SUBLANES, n), F32),
        compiler_params=_cparams(("parallel", "parallel")),
        name="mod_vectors",
    )(c_all, w_mod, b_mod.reshape(depth, 1, n))


def _norm_mod_kernel(x_ref, g_ref, sh_ref, sc_ref, o_ref):
    x = x_ref[0]
    var = jnp.mean(x * x, axis=-1, keepdims=True)
    y = x * lax.rsqrt(var + EPS) * g_ref[0]
    o_ref[0] = (y * (1.0 + sc_ref[0]) + sh_ref[0]).astype(o_ref.dtype)


def _norm_mod(x, gain, mod_rows, layer, k_shift, n_ctx, out_dtype):
    b, t, d = x.shape
    tt = _pick(n_ctx, 256, SUBLANES)
    nct = n_ctx // tt

    def mrow(k):
        return lambda bi, j: ((layer * SUBLANES + jnp.where(j < nct, b, bi)) * N_MOD + k, 0, 0)

    return pl.pallas_call(
        _norm_mod_kernel,
        grid=(b, t // tt),
        in_specs=[
            pl.BlockSpec((1, tt, d), lambda bi, j: (bi, j, 0)),
            pl.BlockSpec((1, 1, d), lambda bi, j: (layer, 0, 0)),
            pl.BlockSpec((1, 1, d), mrow(k_shift)),
            pl.BlockSpec((1, 1, d), mrow(k_shift + 1)),
        ],
        out_specs=pl.BlockSpec((1, tt, d), lambda bi, j: (bi, j, 0)),
        out_shape=jax.ShapeDtypeStruct((b, t, d), out_dtype),
        compiler_params=_cparams(("parallel", "parallel")),
        name="norm_mod",
    )(x, gain.reshape(gain.shape[0], 1, d), mod_rows, mod_rows)


def _final_norm_kernel(x_ref, g_ref, o_ref):
    x = x_ref[0]
    var = jnp.mean(x * x, axis=-1, keepdims=True)
    o_ref[0] = x * lax.rsqrt(var + EPS) * g_ref[...]


def _final_norm(x, gain, n_ctx):
    b, t, d = x.shape
    n_lat = t - n_ctx
    tt = _pick(n_ctx, 256, SUBLANES)
    off = n_ctx // tt
    return pl.pallas_call(
        _final_norm_kernel,
        grid=(b, n_lat // tt),
        in_specs=[
            pl.BlockSpec((1, tt, d), lambda bi, j: (bi, j + off, 0)),
            pl.BlockSpec((1, d), lambda bi, j: (0, 0)),
        ],
        out_specs=pl.BlockSpec((1, tt, d), lambda bi, j: (bi, j, 0)),
        out_shape=jax.ShapeDtypeStruct((b, n_lat, d), F32),
        compiler_params=_cparams(("parallel", "parallel")),
        name="final_norm",
    )(x, gain.reshape(1, d))


def _mm_kernel(a_ref, w_ref, o_ref):
    o_ref[...] = _dot(a_ref[...], w_ref[...]).astype(o_ref.dtype)


def _matmul(a, w, out_dtype):
    m, k = a.shape
    n = w.shape[1]
    tm = _pick(m, 1536, 256) if m > 1536 else m
    tn = _pick(n, 512, LANES)
    return pl.pallas_call(
        _mm_kernel,
        grid=(m // tm, n // tn),
        in_specs=[
            pl.BlockSpec((tm, k), lambda i, j: (i, 0)),
            pl.BlockSpec((k, tn), lambda i, j: (0, j)),
        ],
        out_specs=pl.BlockSpec((tm, tn), lambda i, j: (i, j)),
        out_shape=jax.ShapeDtypeStruct((m, n), out_dtype),
        compiler_params=_cparams(("parallel", "parallel")),
        name="in_proj",
    )(a, w)


def _permute_w_in(w):
    d = w.shape[0]
    dn_cols = 4 * DN_WIDTH + 4 * DN_HEADS
    hg_cols = 5 * HG_WIDTH
    att0 = dn_cols + hg_cols
    parts = [
        w[:, att0:],
        w[:, :4 * DN_WIDTH],
        w[:, dn_cols:att0],
        w[:, 4 * DN_WIDTH:dn_cols],
        jnp.zeros((d, P_WIDTH - C_AB - 4 * DN_HEADS), w.dtype),
    ]
    return jnp.concatenate(parts, axis=1).astype(BF16)


def _att_prep_kernel(q_ref, k_ref, v_ref, cos_ref, sin_ref, qg_ref, kg_ref, qo_ref, ko_ref, vo_ref):
    cos = cos_ref[...]
    sin = sin_ref[...]
    lane = lax.broadcasted_iota(jnp.int32, cos.shape, 1)
    first = (lane % 64) < 32

    def norm_rope(x, g, scale):
        y = x * lax.rsqrt(jnp.mean(x * x, axis=-1, keepdims=True) + EPS) * g
        r = jnp.where(first, pltpu.roll(y, HEAD_DIM - 32, 1), pltpu.roll(y, 32, 1))
        return (y * cos + r * sin) * scale

    q = q_ref[0]
    qg = qg_ref[...]
    for h in range(ATT_Q_HEADS):
        sl = slice(h * HEAD_DIM, (h + 1) * HEAD_DIM)
        qo_ref[0, :, sl] = norm_rope(q[:, sl], qg, HEAD_DIM ** -0.5).astype(qo_ref.dtype)
    k = k_ref[0]
    kg = kg_ref[...]
    for h in range(ATT_KV_HEADS):
        sl = slice(h * HEAD_DIM, (h + 1) * HEAD_DIM)
        ko_ref[0, :, sl] = norm_rope(k[:, sl], kg, 1.0).astype(ko_ref.dtype)
    vo_ref[0] = v_ref[0].astype(vo_ref.dtype)


def _att_prep(p, cos, sin_s, qg, kg):
    b, t, _ = p.shape
    tt = _pick(t, 256, SUBLANES)
    kvb = C_AK // ATT_KV_WIDTH
    return pl.pallas_call(
        _att_prep_kernel,
        grid=(b, t // tt),
        in_specs=[
            pl.BlockSpec((1, tt, ATT_Q_WIDTH), lambda bi, j: (bi, j, 0)),
            pl.BlockSpec((1, tt, ATT_KV_WIDTH), lambda bi, j: (bi, j, kvb)),
            pl.BlockSpec((1, tt, ATT_KV_WIDTH), lambda bi, j: (bi, j, kvb + 1)),
            pl.BlockSpec((tt, HEAD_DIM), lambda bi, j: (j, 0)),
            pl.BlockSpec((tt, HEAD_DIM), lambda bi, j: (j, 0)),
            pl.BlockSpec((1, HEAD_DIM), lambda bi, j: (0, 0)),
            pl.BlockSpec((1, HEAD_DIM), lambda bi, j: (0, 0)),
        ],
        out_specs=[
            pl.BlockSpec((1, tt, ATT_Q_WIDTH), lambda bi, j: (bi, j, 0)),
            pl.BlockSpec((1, tt, ATT_KV_WIDTH), lambda bi, j: (bi, j, 0)),
            pl.BlockSpec((1, tt, ATT_KV_WIDTH), lambda bi, j: (bi, j, 0)),
        ],
        out_shape=[
            jax.ShapeDtypeStruct((b, t, ATT_Q_WIDTH), BF16),
            jax.ShapeDtypeStruct((b, t, ATT_KV_WIDTH), BF16),
            jax.ShapeDtypeStruct((b, t, ATT_KV_WIDTH), BF16),
        ],
        compiler_params=_cparams(("parallel", "parallel")),
        name="att_prep",
    )(p, p, p, cos, sin_s, qg.reshape(1, HEAD_DIM), kg.reshape(1, HEAD_DIM))


def _attn_kernel(q_ref, k_ref, v_ref, o_ref, *, tq, tk, ctx_chunks, all_chunks, ctx_qtiles):
    i = pl.program_id(2)
    q = q_ref[0]
    qs = jnp.concatenate([q[:, g * HEAD_DIM:(g + 1) * HEAD_DIM] for g in range(ATT_GROUP)], axis=0)
    rows = ATT_GROUP * tq
    n_kv = jnp.where(i < ctx_qtiles, ctx_chunks, all_chunks)

    def body(c, carry):
        m, l, acc = carry
        off = pl.multiple_of(c * tk, tk)
        kc = k_ref[0, pl.ds(off, tk), :]
        vc = v_ref[0, pl.ds(off, tk), :]
        s = _dot_nt(qs, kc)
        m_new = jnp.maximum(m, jnp.max(s, axis=1, keepdims=True))
        alpha = jnp.exp(m - m_new)
        pr = jnp.exp(s - m_new)
        l = alpha * l + jnp.sum(pr, axis=1, keepdims=True)
        acc = alpha * acc + _dot(pr.astype(BF16), vc)
        return m_new, l, acc

    init = (jnp.full((rows, 1), -jnp.inf, F32), jnp.zeros((rows, 1), F32), jnp.zeros((rows, HEAD_DIM), F32))
    _, l, acc = lax.fori_loop(0, n_kv, body, init)
    out = acc * (1.0 / l)
    o_ref[0] = jnp.concatenate([out[g * tq:(g + 1) * tq] for g in range(ATT_GROUP)], axis=1).astype(o_ref.dtype)


def _attention(qr, kr, vr, n_ctx):
    b, t, _ = qr.shape
    tq = _pick(n_ctx, 256, SUBLANES)
    tk = _pick(n_ctx, 256, SUBLANES)
    gw = ATT_GROUP * HEAD_DIM
    kern = functools.partial(_attn_kernel, tq=tq, tk=tk, ctx_chunks=n_ctx // tk, all_chunks=t // tk,
                             ctx_qtiles=n_ctx // tq)
    return pl.pallas_call(
        kern,
        grid=(b, ATT_KV_HEADS, t // tq),
        in_specs=[
            pl.BlockSpec((1, tq, gw), lambda bi, h, i: (bi, i, h)),
            pl.BlockSpec((1, t, HEAD_DIM), lambda bi, h, i: (bi, 0, h)),
            pl.BlockSpec((1, t, HEAD_DIM), lambda bi, h, i: (bi, 0, h)),
        ],
        out_specs=pl.BlockSpec((1, tq, gw), lambda bi, h, i: (bi, i, h)),
        out_shape=jax.ShapeDtypeStruct((b, t, ATT_Q_WIDTH), BF16),
        compiler_params=_cparams(("parallel", "parallel", "parallel")),
        name="attention",
    )(qr, kr, vr)


def _chunk_maps(n_ctx, t):
    ncc = n_ctx // CHUNK
    nc = t // CHUNK

    def fwd(i):
        return i

    def bwd(i):
        return jnp.where(i < ncc, ncc - 1 - i, nc - 1 + ncc - i)

    return fwd, bwd, nc


def _gla_chunk(q_raw, f_raw, v, lb, cm, lv, s_ref):
    c = CHUNK
    q = _silu(q_raw)
    one_m_lb = 1.0 - lb
    f_gate = lb + one_m_lb * _sigmoid(f_raw)
    log_f = jnp.log(jnp.maximum(f_gate, GATE_FLOOR))
    key = one_m_lb * _sigmoid(-f_raw)
    hi = log_f.astype(BF16)
    r1 = log_f - hi.astype(F32)
    mid = r1.astype(BF16)
    lo = (r1 - mid.astype(F32)).astype(BF16)
    x = _dot(cm, hi) + _dot(cm, mid) + _dot(cm, lo)
    cum = x[0:c]
    tot = x[7 * c:7 * c + 1]
    att = jnp.where(lv[6] > 0, _dot_nt(q.astype(BF16), key.astype(BF16)), 0.0)
    for n in range(len(_LEVELS)):
        ref = x[(n + 1) * c:(n + 2) * c]
        qs = (q * jnp.exp(jnp.minimum(cum - ref, 0.0))).astype(BF16)
        ks = (key * jnp.exp(jnp.minimum(ref - cum, 0.0))).astype(BF16)
        att = att + jnp.where(lv[n] > 0, _dot_nt(qs, ks), 0.0)
    st = s_ref[...]
    vb = v.astype(BF16)
    o = _dot_nt((q * jnp.exp(cum)).astype(BF16), st.astype(BF16)) + _dot(att.astype(BF16), vb)
    k_dec = (key * jnp.exp(tot - cum)).astype(BF16)
    s_ref[...] = st * jnp.exp(tot) + _dot_tn(vb, k_dec)
    return o


def _hg_kernel(qf_ref, ff_ref, if_ref, qb_ref, fb_ref, ib_ref, lbf_ref, lbb_ref, cm_ref, lv_ref,
               of_ref, ob_ref, s_ref):
    @pl.when(pl.program_id(2) == 0)
    def _():
        s_ref[...] = jnp.zeros_like(s_ref)

    of_ref[0] = _gla_chunk(qf_ref[0], ff_ref[0], if_ref[0], lbf_ref[0], cm_ref[0], lv_ref[0], s_ref.at[0])
    ob_ref[0] = _gla_chunk(qb_ref[0], fb_ref[0], ib_ref[0], lbb_ref[0], cm_ref[1], lv_ref[1], s_ref.at[1])


def _hgrn2(p, lower_bound, n_ctx):
    b, t, _ = p.shape
    fwd, bwd, nc = _chunk_maps(n_ctx, t)
    cm, lv, _, _ = _dir_consts()
    cm = jnp.asarray(cm, BF16)
    lv = jnp.asarray(lv, F32)
    lb = lower_bound.astype(F32).reshape(2 * HG_HEADS, 1, HEAD_DIM)
    hq, hf, hi_ = C_HQ // HEAD_DIM, C_HF // HEAD_DIM, C_HI // HEAD_DIM
    blk = (1, CHUNK, HEAD_DIM)

    def col(base, order):
        return pl.BlockSpec(blk, lambda bi, h, i: (bi, order(i), base + h))

    full = lambda shape: pl.BlockSpec(shape, lambda bi, h, i: (0,) * len(shape))
    return pl.pallas_call(
        _hg_kernel,
        grid=(b, HG_HEADS, nc),
        in_specs=[
            col(hq, fwd), col(hf, fwd), col(hi_, fwd),
            col(hq, bwd), col(hf + HG_HEADS, bwd), col(hi_, bwd),
            pl.BlockSpec((1, 1, HEAD_DIM), lambda bi, h, i: (h, 0, 0)),
            pl.BlockSpec((1, 1, HEAD_DIM), lambda bi, h, i: (HG_HEADS + h, 0, 0)),
            full(cm.shape), full(lv.shape),
        ],
        out_specs=[
            pl.BlockSpec(blk, lambda bi, h, i: (bi, fwd(i), h)),
            pl.BlockSpec(blk, lambda bi, h, i: (bi, bwd(i), h)),
        ],
        out_shape=[jax.ShapeDtypeStruct((b, t, HG_WIDTH), F32)] * 2,
        scratch_shapes=[pltpu.VMEM((2, HEAD_DIM, HEAD_DIM), F32)],
        compiler_params=_cparams(("parallel", "parallel", "arbitrary")),
        name="hgrn2",
    )(p, p, p, p, p, p, lb, lb, cm, lv)


def _dn_prep_kernel(x_ref, prev_ref, next_ref, ab_ref, w_ref, alog_ref, dt_ref,
                    q_ref, k_ref, v_ref, gb_ref, gbt_ref, *, tt, ctx_tiles, n_tiles):
    j = pl.program_id(1)
    seg_start = jnp.logical_or(j == 0, j == ctx_tiles)
    seg_end = jnp.logical_or(j == ctx_tiles - 1, j == n_tiles - 1)
    x = x_ref[0]
    ph = jnp.where(seg_start, 0.0, prev_ref[0])
    nh = jnp.where(seg_end, 0.0, next_ref[0])
    w = w_ref[0]
    r8 = lax.broadcasted_iota(jnp.int32, (SUBLANES, x.shape[1]), 0)
    half = DN_CONV // 2

    def later(s):
        r = pltpu.roll(x, tt - s, 0)
        tail = jnp.where(r8 >= SUBLANES - s, pltpu.roll(nh, SUBLANES - s, 0), r[tt - SUBLANES:])
        return jnp.concatenate([r[:tt - SUBLANES], tail], axis=0)

    def earlier(s):
        r = pltpu.roll(x, s, 0)
        head = jnp.where(r8 < s, pltpu.roll(ph, s, 0), r[:SUBLANES])
        return jnp.concatenate([head, r[SUBLANES:]], axis=0)

    acc = x * w[half:half + 1]
    for s in range(1, half + 1):
        acc = acc + later(s) * w[half + s:half + s + 1] + earlier(s) * w[half - s:half - s + 1]
    y = _silu(acc)
    for h in range(DN_HEADS):
        qh = y[:, h * HEAD_DIM:(h + 1) * HEAD_DIM]
        kh = y[:, DN_WIDTH + h * HEAD_DIM:DN_WIDTH + (h + 1) * HEAD_DIM]
        q_ref[0, h] = qh * lax.rsqrt(jnp.sum(qh * qh, axis=-1, keepdims=True) + EPS) * HEAD_DIM ** -0.5
        k_ref[0, h] = kh * lax.rsqrt(jnp.sum(kh * kh, axis=-1, keepdims=True) + EPS)
        v_ref[0, h] = y[:, 2 * DN_WIDTH + h * HEAD_DIM:2 * DN_WIDTH + (h + 1) * HEAD_DIM]
    a = ab_ref[0]
    z = a + dt_ref[...]
    softplus = jnp.maximum(z, 0.0) + jnp.log1p(jnp.exp(-jnp.abs(z)))
    lane = lax.broadcasted_iota(jnp.int32, a.shape, 1)
    gb = jnp.where(lane < 2 * DN_HEADS, -jnp.exp(alog_ref[...]) * softplus, _sigmoid(a))
    gb_ref[0] = gb
    gbt_ref[0] = gb.T


def _dn_prep(p, conv_w, a_log, dt_bias, n_ctx):
    b, t, _ = p.shape
    tt = _pick(n_ctx, 128, LANES)
    n_tiles = t // tt
    r = tt // SUBLANES
    last8 = t // SUBLANES - 1
    cw = 3 * DN_WIDTH
    cb = C_DQ // cw
    pad = lambda v: jnp.pad(v.astype(F32).reshape(1, -1), ((0, 0), (0, LANES - 2 * DN_HEADS)))
    kern = functools.partial(_dn_prep_kernel, tt=tt, ctx_tiles=n_ctx // tt, n_tiles=n_tiles)
    hshape = jax.ShapeDtypeStruct((b, DN_HEADS, t, HEAD_DIM), F32)
    hspec = pl.BlockSpec((1, DN_HEADS, tt, HEAD_DIM), lambda bi, j: (bi, 0, j, 0))
    return pl.pallas_call(
        kern,
        grid=(b, n_tiles),
        in_specs=[
            pl.BlockSpec((1, tt, cw), lambda bi, j: (bi, j, cb)),
            pl.BlockSpec((1, SUBLANES, cw), lambda bi, j: (bi, jnp.maximum(j * r - 1, 0), cb)),
            pl.BlockSpec((1, SUBLANES, cw), lambda bi, j: (bi, jnp.minimum((j + 1) * r, last8), cb)),
            pl.BlockSpec((1, tt, LANES), lambda bi, j: (bi, j, C_AB // LANES)),
            pl.BlockSpec((1, DN_CONV, cw), lambda bi, j: (0, 0, 0)),
            pl.BlockSpec((1, LANES), lambda bi, j: (0, 0)),
            pl.BlockSpec((1, LANES), lambda bi, j: (0, 0)),
        ],
        out_specs=[
            hspec, hspec, hspec,
            pl.BlockSpec((1, tt, LANES), lambda bi, j: (bi, j, 0)),
            pl.BlockSpec((1, LANES, tt), lambda bi, j: (bi, 0, j)),
        ],
        out_shape=[hshape, hshape, hshape,
                   jax.ShapeDtypeStruct((b, t, LANES), F32),
                   jax.ShapeDtypeStruct((b, LANES, t), F32)],
        compiler_params=_cparams(("parallel", "parallel")),
        name="dn_prep",
    )(p, p, p, p, conv_w.astype(F32)[None], pad(a_log), pad(dt_bias))


def _tri_inverse(a, bd16, off32, off64):
    c = a.shape[0]
    eye = (lax.broadcasted_iota(jnp.int32, (c, c), 0) == lax.broadcasted_iota(jnp.int32, (c, c), 1)).astype(F32)
    n = -(a * bd16)
    r = eye + n
    xp = n
    for _ in range(3):
        xp = _dot3(xp, xp)
        r = r + _dot3(r, xp)
    for off in (off32, off64):
        r = r - _dot3(_dot3(r, a * off), r)
    return r


def _dn_intra_kernel(q_ref, k_ref, v_ref, gb_ref, gbt_ref, incl_ref, lv_ref, bd_ref,
                     u_ref, w_ref, qd_ref, kd_ref, qk_ref):
    h = pl.program_id(1)
    c = CHUNK
    bd16 = bd_ref[...]
    for ci in range(gb_ref.shape[1] // c):
        rows = slice(ci * c, (ci + 1) * c)
        q = q_ref[0, 0, rows]
        k = k_ref[0, 0, rows]
        v = v_ref[0, 0, rows]
        gbt = gb_ref[0, rows]
        lane = lax.broadcasted_iota(jnp.int32, gbt.shape, 1)
        qk_raw = _dot_nt(q.astype(BF16), k.astype(BF16))
        for d in range(2):
            j = d * DN_HEADS + h
            g_col = jnp.sum(jnp.where(lane == j, gbt, 0.0), axis=1, keepdims=True)
            beta = jnp.sum(jnp.where(lane == 2 * DN_HEADS + j, gbt, 0.0), axis=1, keepdims=True)
            g_row = gbt_ref[0, pl.ds(j, 1), :][:, rows]
            incl = incl_ref[d]
            incl_t = incl_ref[1 - d]
            gc_col = jnp.sum(incl * g_row, axis=1, keepdims=True)
            gc_row = jnp.sum(incl_t * g_col, axis=0, keepdims=True)
            tot = jnp.sum(g_row, axis=1, keepdims=True)
            decay = jnp.where(incl > 0, jnp.exp(jnp.minimum(gc_col - gc_row, 0.0)), 0.0)
            kb = k * beta
            strict = incl - lv_ref[d, 6]
            a = strict * _dot_nt(kb.astype(BF16), k.astype(BF16)) * decay
            t_inv = _tri_inverse(a, bd16, lv_ref[d, 1], lv_ref[d, 0]).astype(BF16)
            e_gc = jnp.exp(gc_col)
            u_ref[d, 0, 0, rows] = _dot(t_inv, (v * beta).astype(BF16))
            w_ref[d, 0, 0, rows] = _dot(t_inv, (kb * e_gc).astype(BF16)).astype(w_ref.dtype)
            qd_ref[d, 0, 0, rows] = (q * e_gc).astype(qd_ref.dtype)
            kd_ref[d, 0, 0, rows] = (k * jnp.exp(tot - gc_col)).astype(kd_ref.dtype)
            qk_ref[d, 0, 0, rows] = (qk_raw * decay).astype(qk_ref.dtype)


def _dn_intra(q, k, v, gb, gbt):
    b, nh, t, _ = q.shape
    tb = _pick(t, 128, LANES)
    _, lv, incl, bd16 = _dir_consts()
    lv, incl, bd16 = jnp.asarray(lv), jnp.asarray(incl), jnp.asarray(bd16)
    hspec = pl.BlockSpec((1, 1, tb, HEAD_DIM), lambda bi, h, i: (bi, h, i, 0))
    full = lambda a: pl.BlockSpec(a.shape, lambda bi, h, i: (0,) * a.ndim)
    ospec = pl.BlockSpec((2, 1, 1, tb, HEAD_DIM), lambda bi, h, i: (0, bi, h, i, 0))
    oshape = lambda dt: jax.ShapeDtypeStruct((2, b, nh, t, HEAD_DIM), dt)
    return pl.pallas_call(
        _dn_intra_kernel,
        grid=(b, nh, t // tb),
        in_specs=[
            hspec, hspec, hspec,
            pl.BlockSpec((1, tb, LANES), lambda bi, h, i: (bi, i, 0)),
            pl.BlockSpec((1, LANES, tb), lambda bi, h, i: (bi, 0, i)),
            full(incl), full(lv), full(bd16),
        ],
        out_specs=[ospec, ospec, ospec, ospec,
                   pl.BlockSpec((2, 1, 1, tb, CHUNK), lambda bi, h, i: (0, bi, h, i, 0))],
        out_shape=[oshape(F32), oshape(BF16), oshape(BF16), oshape(BF16),
                   jax.ShapeDtypeStruct((2, b, nh, t, CHUNK), BF16)],
        compiler_params=_cparams(("parallel", "parallel", "parallel")),
        name="dn_intra",
    )(q, k, v, gb, gbt, incl, lv, bd16)


def _dn_scan_kernel(*refs):
    ins = refs[:12]
    of_ref, ob_ref, s_ref = refs[12:]

    @pl.when(pl.program_id(1) == 0)
    def _():
        s_ref[...] = jnp.zeros_like(s_ref)

    for d, o_ref in ((0, of_ref), (1, ob_ref)):
        u_ref, w_ref, qd_ref, kd_ref, qk_ref, gb_ref = ins[6 * d:6 * d + 6]
        last = jnp.exp(jnp.sum(gb_ref[0], axis=0, keepdims=True))
        for h in range(DN_HEADS):
            s = s_ref[d, h]
            wq = jnp.concatenate([w_ref[0, 0, h], qd_ref[0, 0, h]], axis=0)
            r = _dot(wq, s.astype(BF16))
            v_new = (u_ref[0, 0, h] - r[:CHUNK]).astype(BF16)
            o_ref[0, :, h * HEAD_DIM:(h + 1) * HEAD_DIM] = r[CHUNK:] + _dot(qk_ref[0, 0, h], v_new)
            j = d * DN_HEADS + h
            s_ref[d, h] = s * last[:, j:j + 1] + _dot_tn(kd_ref[0, 0, h], v_new)


def _dn_scan(u, w, qd, kd, qk, gb, n_ctx):
    _, b, nh, t, _ = u.shape
    fwd, bwd, nc = _chunk_maps(n_ctx, t)

    def specs(d, order):
        hs = lambda width: pl.BlockSpec((1, 1, nh, CHUNK, width), lambda bi, i: (d, bi, 0, order(i), 0))
        return [hs(HEAD_DIM)] * 4 + [hs(CHUNK), pl.BlockSpec((1, CHUNK, LANES), lambda bi, i: (bi, order(i), 0))]

    args = (u, w, qd, kd, qk, gb)
    return pl.pallas_call(
        _dn_scan_kernel,
        grid=(b, nc),
        in_specs=specs(0, fwd) + specs(1, bwd),
        out_specs=[
            pl.BlockSpec((1, CHUNK, DN_WIDTH), lambda bi, i: (bi, fwd(i), 0)),
            pl.BlockSpec((1, CHUNK, DN_WIDTH), lambda bi, i: (bi, bwd(i), 0)),
        ],
        out_shape=[jax.ShapeDtypeStruct((b, t, DN_WIDTH), F32)] * 2,
        scratch_shapes=[pltpu.VMEM((2, nh, HEAD_DIM, HEAD_DIM), F32)],
        compiler_params=_cparams(("parallel", "arbitrary")),
        name="dn_scan",
    )(*args, *args)


def _mix_finish_kernel(of_ref, ob_ref, z_ref, g_ref, o_ref):
    g = g_ref[...]
    for h in range(of_ref.shape[2] // HEAD_DIM):
        sl = slice(h * HEAD_DIM, (h + 1) * HEAD_DIM)
        o = of_ref[0, :, sl] + ob_ref[0, :, sl]
        y = o * lax.rsqrt(jnp.mean(o * o, axis=-1, keepdims=True) + EPS) * g
        o_ref[0, :, sl] = (y * _silu(z_ref[0, :, sl])).astype(o_ref.dtype)


def _mix_finish(o_f, o_b, p, gate_col, gain):
    b, t, width = o_f.shape
    tt = _pick(t, 256, SUBLANES)
    spec = pl.BlockSpec((1, tt, width), lambda bi, j: (bi, j, 0))
    return pl.pallas_call(
        _mix_finish_kernel,
        grid=(b, t // tt),
        in_specs=[spec, spec,
                  pl.BlockSpec((1, tt, width), lambda bi, j: (bi, j, gate_col // width)),
                  pl.BlockSpec((1, HEAD_DIM), lambda bi, j: (0, 0))],
        out_specs=spec,
        out_shape=jax.ShapeDtypeStruct((b, t, width), BF16),
        compiler_params=_cparams(("parallel", "parallel")),
        name="mix_finish",
    )(o_f, o_b, p, gain.astype(F32).reshape(1, HEAD_DIM))


def _out_proj_kernel(a1_ref, a2_ref, a3_ref, w1_ref, w2_ref, w3_ref, x_ref, g_ref, o_ref):
    y = _dot(a1_ref[0], w1_ref[...]) + _dot(a2_ref[0], w2_ref[...]) + _dot(a3_ref[0], w3_ref[...])
    o_ref[0] = x_ref[0] + g_ref[0] * y


def _out_proj(oa, ob, og, w_out, x, mod_rows, layer, n_ctx):
    b, t, d = x.shape
    tm = _pick(n_ctx, 256, SUBLANES)
    tn = _pick(d, 1024, LANES)
    nct = n_ctx // tm
    w = w_out.astype(BF16)
    return pl.pallas_call(
        _out_proj_kernel,
        grid=(d // tn, b, t // tm),
        in_specs=[
            pl.BlockSpec((1, tm, DN_WIDTH), lambda n, bi, i: (bi, i, 0)),
            pl.BlockSpec((1, tm, HG_WIDTH), lambda n, bi, i: (bi, i, 0)),
            pl.BlockSpec((1, tm, ATT_Q_WIDTH), lambda n, bi, i: (bi, i, 0)),
            pl.BlockSpec((DN_WIDTH, tn), lambda n, bi, i: (0, n)),
            pl.BlockSpec((HG_WIDTH, tn), lambda n, bi, i: (1, n)),
            pl.BlockSpec((ATT_Q_WIDTH, tn), lambda n, bi, i: (1, n)),
            pl.BlockSpec((1, tm, tn), lambda n, bi, i: (bi, i, n)),
            pl.BlockSpec((1, 1, tn), lambda n, bi, i: (
                (layer * SUBLANES + jnp.where(i < nct, b, bi)) * N_MOD + 2, 0, n)),
        ],
        out_specs=pl.BlockSpec((1, tm, tn), lambda n, bi, i: (bi, i, n)),
        out_shape=jax.ShapeDtypeStruct((b, t, d), F32),
        compiler_params=_cparams(("parallel", "parallel", "parallel")),
        name="out_proj",
    )(oa, ob, og, w, w, w, x, mod_rows)


def _router_kernel(h_ref, w_ref, b_ref, idx_ref, wt_ref, rank_ref, cnt_ref, carry_ref):
    i = pl.program_id(0)

    @pl.when(i == 0)
    def _():
        carry_ref[...] = jnp.zeros_like(carry_ref)

    tm = h_ref.shape[0]
    neg = -jnp.inf
    logits = _dot(h_ref[...].astype(BF16), w_ref[...])
    scores = _sigmoid(logits)
    lane = lax.broadcasted_iota(jnp.int32, scores.shape, 1)
    valid = lane < N_EXPERTS
    grp = lane // GROUP_SIZE
    big = jnp.int32(1 << 20)
    sel = jnp.where(valid, scores + b_ref[...], neg)
    rmax = lambda a: jnp.max(a, axis=1, keepdims=True)
    rmin = lambda a: jnp.min(a, axis=1, keepdims=True)

    grp_score = jnp.full(scores.shape, neg, F32)
    for g in range(N_GROUPS):
        in_g = grp == g
        cur = jnp.where(in_g, sel, neg)
        m1 = rmax(cur)
        i1 = rmin(jnp.where(cur == m1, lane, big))
        m2 = rmax(jnp.where(lane == i1, neg, cur))
        grp_score = jnp.where(in_g, m1 + m2, grp_score)
    keep = jnp.zeros(scores.shape, jnp.bool_)
    cur = grp_score
    for _ in range(TOPK_GROUPS):
        m = rmax(cur)
        gsel = rmin(jnp.where(cur == m, grp, big))
        hit = grp == gsel
        keep = jnp.logical_or(keep, hit)
        cur = jnp.where(hit, neg, cur)
    cur = jnp.where(keep, sel, neg)
    idx_out = jnp.zeros(scores.shape, jnp.int32)
    w_out = jnp.zeros(scores.shape, F32)
    onehots = []
    for k in range(TOP_K):
        m = rmax(cur)
        ik = rmin(jnp.where(cur == m, lane, big))
        hit = lane == ik
        sk = jnp.sum(jnp.where(hit, scores, 0.0), axis=1, keepdims=True)
        idx_out = jnp.where(lane == k, ik, idx_out)
        w_out = jnp.where(lane == k, sk, w_out)
        onehots.append(hit)
        cur = jnp.where(hit, neg, cur)
    w_out = w_out * (ROUTED_SCALE / jnp.sum(w_out, axis=1, keepdims=True))
    chosen = onehots[0]
    for oh in onehots[1:]:
        chosen = jnp.logical_or(chosen, oh)
    e_mat = jnp.where(chosen, 1.0, 0.0)
    r_i = lax.broadcasted_iota(jnp.int32, (tm, tm), 0)
    c_i = lax.broadcasted_iota(jnp.int32, (tm, tm), 1)
    lower = jnp.where(c_i < r_i, 1.0, 0.0).astype(BF16)
    before = _dot(lower, e_mat.astype(BF16)) + carry_ref[...]
    rank_out = jnp.zeros(scores.shape, jnp.int32)
    for k in range(TOP_K):
        rk = jnp.sum(jnp.where(onehots[k], before, 0.0), axis=1, keepdims=True)
        rank_out = jnp.where(lane == k, rk.astype(jnp.int32), rank_out)
    carry_ref[...] = carry_ref[...] + jnp.sum(e_mat, axis=0, keepdims=True)
    idx_ref[...] = idx_out
    wt_ref[...] = w_out
    rank_ref[...] = rank_out
    cnt_ref[...] = jnp.broadcast_to(carry_ref[...], cnt_ref.shape).astype(jnp.int32)


def _router(h, w_router, bias):
    n, d = h.shape
    tm = _pick(n, 256, SUBLANES)
    w = jnp.pad(w_router, ((0, 0), (0, LANES - N_EXPERTS))).astype(BF16)
    bz = jnp.pad(bias.astype(F32).reshape(1, -1), ((0, 0), (0, LANES - N_EXPERTS)))
    tile = pl.BlockSpec((tm, LANES), lambda i: (i, 0))
    return pl.pallas_call(
        _router_kernel,
        grid=(n // tm,),
        in_specs=[
            pl.BlockSpec((tm, d), lambda i: (i, 0)),
            pl.BlockSpec((d, LANES), lambda i: (0, 0)),
            pl.BlockSpec((1, LANES), lambda i: (0, 0)),
        ],
        out_specs=[tile, tile, tile, pl.BlockSpec((SUBLANES, LANES), lambda i: (0, 0))],
        out_shape=[
            jax.ShapeDtypeStruct((n, LANES), jnp.int32),
            jax.ShapeDtypeStruct((n, LANES), F32),
            jax.ShapeDtypeStruct((n, LANES), jnp.int32),
            jax.ShapeDtypeStruct((SUBLANES, LANES), jnp.int32),
        ],
        scratch_shapes=[pltpu.VMEM((1, LANES), F32)],
        compiler_params=_cparams(("arbitrary",)),
        name="router",
    )(h, w, bz)


def _expert_kernel(be_ref, nu_ref, tok_ref, nxt_ref, h_hbm, wg_ref, wu_ref, wd_ref, y_ref, xbuf, sem):
    i = pl.program_id(0)
    n_used = nu_ref[0]
    slot = i % 2

    def row_copy(tok, slot_, r):
        return pltpu.make_async_copy(h_hbm.at[pl.ds(tok, 1)], xbuf.at[slot_, pl.ds(r, 1)], sem.at[slot_])

    def start_gather(idx_ref, slot_):
        def body(r, c):
            row_copy(idx_ref[0, 0, r], slot_, r).start()
            return c
        lax.fori_loop(0, MOE_ROWS, body, 0)

    @pl.when(jnp.logical_and(i == 0, n_used > 0))
    def _():
        start_gather(tok_ref, 0)

    @pl.when(i + 1 < n_used)
    def _():
        start_gather(nxt_ref, 1 - slot)

    @pl.when(i < n_used)
    def _():
        def body(r, c):
            row_copy(0, slot, r).wait()
            return c
        lax.fori_loop(0, MOE_ROWS, body, 0)
        x = xbuf[slot].astype(BF16)
        a = _silu(_dot(x, wg_ref[0])) * _dot(x, wu_ref[0])
        y_ref[...] = _dot(a.astype(BF16), wd_ref[0])

    @pl.when(i >= n_used)
    def _():
        y_ref[...] = jnp.zeros_like(y_ref)


def _routed_experts(h, row_tok, block_e, n_used, wg, wu, wd):
    n, d = h.shape
    n_blocks = block_e.shape[0]
    ff = wg.shape[2]
    tok3 = row_tok.reshape(n_blocks, 1, MOE_ROWS)
    grid_spec = pltpu.PrefetchScalarGridSpec(
        num_scalar_prefetch=2,
        grid=(n_blocks,),
        in_specs=[
            pl.BlockSpec((1, 1, MOE_ROWS), lambda i, be, nu: (i, 0, 0), memory_space=pltpu.SMEM),
            pl.BlockSpec((1, 1, MOE_ROWS), lambda i, be, nu: (jnp.minimum(i + 1, n_blocks - 1), 0, 0),
                         memory_space=pltpu.SMEM),
            pl.BlockSpec(memory_space=pl.ANY),
            pl.BlockSpec((1, d, ff), lambda i, be, nu: (be[i], 0, 0)),
            pl.BlockSpec((1, d, ff), lambda i, be, nu: (be[i], 0, 0)),
            pl.BlockSpec((1, ff, d), lambda i, be, nu: (be[i], 0, 0)),
        ],
        out_specs=pl.BlockSpec((MOE_ROWS, d), lambda i, be, nu: (i, 0)),
        scratch_shapes=[pltpu.VMEM((2, MOE_ROWS, d), F32), pltpu.SemaphoreType.DMA((2,))],
    )
    return pl.pallas_call(
        _expert_kernel,
        grid_spec=grid_spec,
        out_shape=jax.ShapeDtypeStruct((n_blocks * MOE_ROWS, d), F32),
        compiler_params=_cparams(("arbitrary",)),
        name="routed_experts",
    )(block_e, n_used, tok3, tok3, h, wg, wu, wd)


def _combine_kernel(pos_ref, nxt_ref, y_hbm, wt_ref, h_ref, x_ref, g_ref, wsg_ref, wsu_ref, wsd_ref,
                    o_ref, ybuf, sem, *, tt):
    i = pl.program_id(0)
    n = pl.num_programs(0)
    slot = i % 2
    rows = tt * TOP_K

    def row_copy(src, slot_, r):
        return pltpu.make_async_copy(y_hbm.at[pl.ds(src, 1)], ybuf.at[slot_, pl.ds(r, 1)], sem.at[slot_])

    def start_gather(idx_ref, slot_):
        def body(r, c):
            row_copy(idx_ref[0, 0, r], slot_, r).start()
            return c
        lax.fori_loop(0, rows, body, 0)

    @pl.when(i == 0)
    def _():
        start_gather(pos_ref, 0)

    @pl.when(i + 1 < n)
    def _():
        start_gather(nxt_ref, 1 - slot)

    def wbody(r, c):
        row_copy(0, slot, r).wait()
        return c
    lax.fori_loop(0, rows, wbody, 0)

    wt = wt_ref[...]
    routed = jnp.zeros(x_ref.shape, F32)
    for k in range(TOP_K):
        routed = routed + wt[:, k:k + 1] * ybuf[slot, k * tt:(k + 1) * tt, :]
    hb = h_ref[...].astype(BF16)
    a = _silu(_dot(hb, wsg_ref[...])) * _dot(hb, wsu_ref[...])
    shared = _dot(a.astype(BF16), wsd_ref[...])
    o_ref[...] = x_ref[...] + g_ref[0] * (routed + shared)


def _combine(y, pos, wts, h, x, mod_rows, layer, wsg, wsu, wsd, batch, n_ctx):
    n, d = x.shape
    t = n // batch
    tt = _pick(n_ctx, 64, SUBLANES)
    n_tiles = n // tt
    tiles_per_b = t // tt
    nct = n_ctx // tt
    ff = wsg.shape[1]
    pos3 = pos.reshape(n_tiles, tt, TOP_K).transpose(0, 2, 1).reshape(n_tiles, 1, tt * TOP_K)
    row = lambda i: (layer * SUBLANES + jnp.where(i % tiles_per_b < nct, batch, i // tiles_per_b)) * N_MOD + 5
    kern = functools.partial(_combine_kernel, tt=tt)
    tile = pl.BlockSpec((tt, d), lambda i: (i, 0))
    return pl.pallas_call(
        kern,
        grid=(n_tiles,),
        in_specs=[
            pl.BlockSpec((1, 1, tt * TOP_K), lambda i: (i, 0, 0), memory_space=pltpu.SMEM),
            pl.BlockSpec((1, 1, tt * TOP_K), lambda i: (jnp.minimum(i + 1, n_tiles - 1), 0, 0),
                         memory_space=pltpu.SMEM),
            pl.BlockSpec(memory_space=pl.ANY),
            pl.BlockSpec((tt, LANES), lambda i: (i, 0)),
            tile, tile,
            pl.BlockSpec((1, 1, d), lambda i: (row(i), 0, 0)),
            pl.BlockSpec((d, ff), lambda i: (0, 0)),
            pl.BlockSpec((d, ff), lambda i: (0, 0)),
            pl.BlockSpec((ff, d), lambda i: (0, 0)),
        ],
        out_specs=tile,
        out_shape=jax.ShapeDtypeStruct((n, d), F32),
        scratch_shapes=[pltpu.VMEM((2, tt * TOP_K, d), F32), pltpu.SemaphoreType.DMA((2,))],
        compiler_params=_cparams(("arbitrary",)),
        name="moe_combine",
    )(pos3, pos3, y, wts, h, x, mod_rows, wsg, wsu, wsd)


def _moe(h, x, mod_rows, layer, w_router, router_bias, wg, wu, wd, wsg, wsu, wsd, batch, n_ctx):
    n, d = h.shape
    idx, wts, rank, counts = _router(h, w_router, router_bias)
    idx = idx[:, :TOP_K]
    rank = rank[:, :TOP_K]
    counts = counts[0, :N_EXPERTS]
    padded = (counts + MOE_ROWS - 1) // MOE_ROWS * MOE_ROWS
    pad_end = jnp.cumsum(padded)
    pad_start = pad_end - padded
    pos = pad_start[idx] + rank
    n_blocks = -(-(n * TOP_K) // MOE_ROWS) + N_EXPERTS
    tok = jnp.broadcast_to(jnp.arange(n, dtype=jnp.int32)[:, None], pos.shape)
    row_tok = jnp.zeros((n_blocks * MOE_ROWS,), jnp.int32).at[pos.reshape(-1)].set(tok.reshape(-1))
    block_e = jnp.minimum(
        jnp.searchsorted(pad_end, jnp.arange(n_blocks, dtype=jnp.int32) * MOE_ROWS, side="right"),
        N_EXPERTS - 1).astype(jnp.int32)
    n_used = (pad_end[-1] // MOE_ROWS).astype(jnp.int32).reshape(1)
    y = _routed_experts(h, row_tok, block_e, n_used, wg, wu, wd)
    return _combine(y, pos.astype(jnp.int32), wts, h, x, mod_rows, layer, wsg, wsu, wsd, batch, n_ctx)


def _rope_tables(n_ctx, n_lat):
    rows = n_lat // GRID_W
    row, col = jnp.meshgrid(jnp.arange(rows), jnp.arange(GRID_W), indexing="ij")
    half = HEAD_DIM // 2
    inv_freq = ROPE_THETA ** (-jnp.arange(0, half, 2, dtype=F32) / half)
    ang_r = row.reshape(-1, 1).astype(F32) * inv_freq
    ang_c = col.reshape(-1, 1).astype(F32) * inv_freq
    ang = jnp.concatenate([ang_r, ang_r, ang_c, ang_c], axis=-1)
    cos = jnp.concatenate([jnp.ones((n_ctx, HEAD_DIM), F32), jnp.cos(ang)], axis=0)
    sin = jnp.concatenate([jnp.zeros((n_ctx, HEAD_DIM), F32), jnp.sin(ang)], axis=0)
    first = (jnp.arange(HEAD_DIM) % 64) < 32
    return cos, jnp.where(first[None, :], -sin, sin)


def kernel(x, c, ctx, c_ctx, w_mod, b_mod, norm1_g, norm2_g, w_in, w_out, dn_conv_w, dn_a_log, dn_dt_bias, dn_norm_g, hg_lb_logits, hg_norm_g, att_q_norm_g, att_k_norm_g, w_router, router_bias, w_exp_gate, w_exp_up, w_exp_down, w_sh_gate, w_sh_up, w_sh_down, final_norm_g):
    bsz, n_lat, d = x.shape
    n_ctx = ctx.shape[1]
    depth = w_in.shape[0]
    t = n_ctx + n_lat
    assert bsz + 1 <= SUBLANES and n_ctx % CHUNK == 0 and n_lat % CHUNK == 0 and n_lat % GRID_W == 0

    cos, sin_s = _rope_tables(n_ctx, n_lat)
    lb_p = jax.nn.softmax(hg_lb_logits.astype(F32), axis=0)
    lower_bounds = jnp.cumsum(lb_p, axis=0) - lb_p[0]

    c_all = jnp.concatenate([c, c_ctx[None, :], jnp.zeros((SUBLANES - bsz - 1, d), c.dtype)], axis=0)
    mod = _mod_vectors(c_all, w_mod, b_mod)
    mod_rows = mod.reshape(depth * SUBLANES * N_MOD, 1, d)

    xs = jnp.concatenate([ctx, x], axis=1)
    for l in range(depth):
        h1 = _norm_mod(xs, norm1_g, mod_rows, l, 0, n_ctx, BF16)
        p = _matmul(h1.reshape(bsz * t, d), _permute_w_in(w_in[l]), F32).reshape(bsz, t, P_WIDTH)
        qr, kr, vr = _att_prep(p, cos, sin_s, att_q_norm_g[l], att_k_norm_g[l])
        og = _attention(qr, kr, vr, n_ctx)
        hf, hb = _hgrn2(p, lower_bounds[l], n_ctx)
        ob = _mix_finish(hf, hb, p, C_HG, hg_norm_g[l])
        dq, dk, dv, gb, gbt = _dn_prep(p, dn_conv_w[l], dn_a_log[l], dn_dt_bias[l], n_ctx)
        u, w, qd, kd, qk = _dn_intra(dq, dk, dv, gb, gbt)
        df, db = _dn_scan(u, w, qd, kd, qk, gb, n_ctx)
        oa = _mix_finish(df, db, p, C_DZ, dn_norm_g[l])
        xs = _out_proj(oa, ob, og, w_out[l], xs, mod_rows, l, n_ctx)
        h2 = _norm_mod(xs, norm2_g, mod_rows, l, 3, n_ctx, F32).reshape(bsz * t, d)
        xs = _moe(h2, xs.reshape(bsz * t, d), mod_rows, l, w_router[l], router_bias[l],
                  w_exp_gate[l].astype(BF16), w_exp_up[l].astype(BF16), w_exp_down[l].astype(BF16),
                  w_sh_gate[l].astype(BF16), w_sh_up[l].astype(BF16), w_sh_down[l].astype(BF16),
                  bsz, n_ctx).reshape(bsz, t, d)
    return _final_norm(xs, final_norm_g, n_ctx)
```

```python
import functools

import numpy as np
import jax
import jax.numpy as jnp
from jax import lax
from jax.experimental import pallas as pl
from jax.experimental.pallas import tpu as pltpu

F32 = jnp.float32
BF16 = jnp.bfloat16

EPS = 1e-6
GATE_FLOOR = 1e-30
HEAD_DIM = 128
DN_HEADS = 8
DN_WIDTH = DN_HEADS * HEAD_DIM
DN_CONV = 5
HG_HEADS = 8
HG_WIDTH = HG_HEADS * HEAD_DIM
ATT_Q_HEADS = 16
ATT_KV_HEADS = 4
ATT_GROUP = ATT_Q_HEADS // ATT_KV_HEADS
ATT_Q_WIDTH = ATT_Q_HEADS * HEAD_DIM
ATT_KV_WIDTH = ATT_KV_HEADS * HEAD_DIM
GRID_W = 64
ROPE_THETA = 10000.0
CHUNK = 64
N_EXPERTS = 64
TOP_K = 8
N_GROUPS = 8
TOPK_GROUPS = 4
GROUP_SIZE = N_EXPERTS // N_GROUPS
ROUTED_SCALE = 2.5
N_MOD = 6
LOG2E = 1.4426950408889634

C_AQ = 0
C_AK = C_AQ + ATT_Q_WIDTH
C_AV = C_AK + ATT_KV_WIDTH
C_DQ = C_AV + ATT_KV_WIDTH
C_DZ = C_DQ + 3 * DN_WIDTH
C_HQ = C_DZ + DN_WIDTH
C_HF = C_HQ + HG_WIDTH
C_HI = C_HF + 2 * HG_WIDTH
C_HG = C_HI + HG_WIDTH
C_AB = C_HG + HG_WIDTH
LANES = 128
SUBLANES = 8
P_WIDTH = C_AB + 4 * LANES

MOE_ROWS = 256
DMA_UNROLL = 16
COMBINE_COLS = 512
VMEM_LIMIT = 56 * 1024 * 1024


def _cparams(sem, vmem=VMEM_LIMIT):
    return pltpu.CompilerParams(dimension_semantics=sem, vmem_limit_bytes=vmem)


def _pick(n, cap, mult):
    if n <= cap:
        return n
    for t in range(cap - cap % mult, 0, -mult):
        if n % t == 0:
            return t
    raise ValueError(f"no tile for {n} (cap {cap}, multiple of {mult})")


def _silu(x):
    return x * (1.0 / (1.0 + jnp.exp(-x)))


def _sigmoid(x):
    return 1.0 / (1.0 + jnp.exp(-x))


def _dot(a, b):
    return jnp.dot(a, b, preferred_element_type=F32)


def _dot_nt(a, b):
    return lax.dot_general(a, b, (((1,), (1,)), ((), ())), preferred_element_type=F32)


def _dot_tn(a, b):
    return lax.dot_general(a, b, (((0,), (0,)), ((), ())), preferred_element_type=F32)


def _split2(a):
    hi = a.astype(BF16)
    lo = (a - hi.astype(F32)).astype(BF16)
    return hi, lo


def _dot3(a, b):
    ah, al = _split2(a)
    bh, bl = _split2(b)
    return _dot(ah, bh) + _dot(ah, bl) + _dot(al, bh)


_LEVELS = (32, 16, 8, 4, 2, 1)


@functools.lru_cache(maxsize=None)
def _dir_consts():
    c = CHUNK
    t = np.arange(c)
    incl = [(t[None, :] <= t[:, None]), (t[None, :] >= t[:, None])]
    cms, masks = [], []
    for d in range(2):
        md = incl[d].astype(np.float32)
        blocks, mks = [md], []
        for b in _LEVELS:
            blk = t // (2 * b)
            ref = blk * 2 * b + (b - 1 if d == 0 else b)
            blocks.append(md[ref, :])
            same = blk[:, None] == blk[None, :]
            late = (t % (2 * b)) >= b
            if d == 0:
                mks.append(same & late[:, None] & ~late[None, :])
            else:
                mks.append(same & ~late[:, None] & late[None, :])
        mks.append(np.eye(c, dtype=bool))
        blocks.append(np.ones((c, c), np.float32))
        cms.append(np.concatenate(blocks, 0))
        masks.append(np.stack(mks).astype(np.float32))
    cm = np.stack(cms)
    lv = np.stack(masks)
    incl_f = np.stack([m.astype(np.float32) for m in incl])
    bd16 = ((t[:, None] // 16) == (t[None, :] // 16)).astype(np.float32)
    return cm, lv, incl_f, bd16


def _mod_kernel(c_ref, w_ref, b_ref, o_ref):
    a = _silu(c_ref[...]).astype(BF16)
    o_ref[0] = _dot(a, w_ref[0].astype(BF16)) + b_ref[0]


def _mod_vectors(c_all, w_mod, b_mod):
    depth, d, n = w_mod.shape
    tn = _pick(n, 512, LANES)
    return pl.pallas_call(
        _mod_kernel,
        grid=(depth, n // tn),
        in_specs=[
            pl.BlockSpec((SUBLANES, d), lambda l, j: (0, 0)),
            pl.BlockSpec((1, d, tn), lambda l, j: (l, 0, j)),
            pl.BlockSpec((1, 1, tn), lambda l, j: (l, 0, j)),
        ],
        out_specs=pl.BlockSpec((1, SUBLANES, tn), lambda l, j: (l, 0, j)),
        out_shape=jax.ShapeDtypeStruct((depth, SUBLANES, n), F32),
        compiler_params=_cparams(("parallel", "parallel")),
        name="mod_vectors",
    )(c_all, w_mod, b_mod.reshape(depth, 1, n))


def _norm_mod_kernel(x_ref, g_ref, sh_ref, sc_ref, o_ref):
    x = x_ref[0]
    var = jnp.mean(x * x, axis=-1, keepdims=True)
    y = x * lax.rsqrt(var + EPS) * g_ref[0]
    o_ref[0] = (y * (1.0 + sc_ref[0]) + sh_ref[0]).astype(o_ref.dtype)


def _norm_mod(x, gain, mod_rows, layer, k_shift, n_ctx, out_dtype):
    b, t, d = x.shape
    tt = _pick(n_ctx, 256, SUBLANES)
    nct = n_ctx // tt

    def mrow(k):
        return lambda bi, j: ((layer * SUBLANES + jnp.where(j < nct, b, bi)) * N_MOD + k, 0, 0)

    return pl.pallas_call(
        _norm_mod_kernel,
        grid=(b, t // tt),
        in_specs=[
            pl.BlockSpec((1, tt, d), lambda bi, j: (bi, j, 0)),
            pl.BlockSpec((1, 1, d), lambda bi, j: (layer, 0, 0)),
            pl.BlockSpec((1, 1, d), mrow(k_shift)),
            pl.BlockSpec((1, 1, d), mrow(k_shift + 1)),
        ],
        out_specs=pl.BlockSpec((1, tt, d), lambda bi, j: (bi, j, 0)),
        out_shape=jax.ShapeDtypeStruct((b, t, d), out_dtype),
        compiler_params=_cparams(("parallel", "parallel")),
        name="norm_mod",
    )(x, gain.reshape(gain.shape[0], 1, d), mod_rows, mod_rows)


def _final_norm_kernel(x_ref, g_ref, o_ref):
    x = x_ref[0]
    var = jnp.mean(x * x, axis=-1, keepdims=True)
    o_ref[0] = x * lax.rsqrt(var + EPS) * g_ref[...]


def _final_norm(x, gain, n_ctx):
    b, t, d = x.shape
    n_lat = t - n_ctx
    tt = _pick(n_ctx, 256, SUBLANES)
    off = n_ctx // tt
    return pl.pallas_call(
        _final_norm_kernel,
        grid=(b, n_lat // tt),
        in_specs=[
            pl.BlockSpec((1, tt, d), lambda bi, j: (bi, j + off, 0)),
            pl.BlockSpec((1, d), lambda bi, j: (0, 0)),
        ],
        out_specs=pl.BlockSpec((1, tt, d), lambda bi, j: (bi, j, 0)),
        out_shape=jax.ShapeDtypeStruct((b, n_lat, d), F32),
        compiler_params=_cparams(("parallel", "parallel")),
        name="final_norm",
    )(x, gain.reshape(1, d))


def _mm_kernel(a_ref, w_ref, o_ref):
    o_ref[...] = _dot(a_ref[...], w_ref[...]).astype(o_ref.dtype)


def _matmul(a, w, out_dtype):
    m, k = a.shape
    n = w.shape[1]
    tm = _pick(m, 1536, 256) if m > 1536 else m
    tn = _pick(n, 512, LANES)
    return pl.pallas_call(
        _mm_kernel,
        grid=(m // tm, n // tn),
        in_specs=[
            pl.BlockSpec((tm, k), lambda i, j: (i, 0)),
            pl.BlockSpec((k, tn), lambda i, j: (0, j)),
        ],
        out_specs=pl.BlockSpec((tm, tn), lambda i, j: (i, j)),
        out_shape=jax.ShapeDtypeStruct((m, n), out_dtype),
        compiler_params=_cparams(("parallel", "parallel")),
        name="in_proj",
    )(a, w)


def _permute_w_in(w):
    d = w.shape[0]
    dn_cols = 4 * DN_WIDTH + 4 * DN_HEADS
    hg_cols = 5 * HG_WIDTH
    att0 = dn_cols + hg_cols
    parts = [
        w[:, att0:],
        w[:, :4 * DN_WIDTH],
        w[:, dn_cols:att0],
        w[:, 4 * DN_WIDTH:dn_cols],
        jnp.zeros((d, P_WIDTH - C_AB - 4 * DN_HEADS), w.dtype),
    ]
    return jnp.concatenate(parts, axis=1).astype(BF16)


def _att_prep_kernel(q_ref, k_ref, v_ref, cos_ref, sin_ref, qg_ref, kg_ref, qo_ref, ko_ref, vo_ref):
    cos = cos_ref[...]
    sin = sin_ref[...]
    lane = lax.broadcasted_iota(jnp.int32, cos.shape, 1)
    first = (lane % 64) < 32

    def norm_rope(x, g, scale):
        y = x * lax.rsqrt(jnp.mean(x * x, axis=-1, keepdims=True) + EPS) * g
        r = jnp.where(first, pltpu.roll(y, HEAD_DIM - 32, 1), pltpu.roll(y, 32, 1))
        return (y * cos + r * sin) * scale

    q = q_ref[0]
    qg = qg_ref[...]
    for h in range(ATT_Q_HEADS):
        sl = slice(h * HEAD_DIM, (h + 1) * HEAD_DIM)
        qo_ref[0, :, sl] = norm_rope(q[:, sl], qg, HEAD_DIM ** -0.5 * LOG2E).astype(qo_ref.dtype)
    k = k_ref[0]
    kg = kg_ref[...]
    for h in range(ATT_KV_HEADS):
        sl = slice(h * HEAD_DIM, (h + 1) * HEAD_DIM)
        ko_ref[0, :, sl] = norm_rope(k[:, sl], kg, 1.0).astype(ko_ref.dtype)
    vo_ref[0] = v_ref[0].astype(vo_ref.dtype)


def _att_prep(p, cos, sin_s, qg, kg):
    b, t, _ = p.shape
    tt = _pick(t, 256, SUBLANES)
    kvb = C_AK // ATT_KV_WIDTH
    return pl.pallas_call(
        _att_prep_kernel,
        grid=(b, t // tt),
        in_specs=[
            pl.BlockSpec((1, tt, ATT_Q_WIDTH), lambda bi, j: (bi, j, 0)),
            pl.BlockSpec((1, tt, ATT_KV_WIDTH), lambda bi, j: (bi, j, kvb)),
            pl.BlockSpec((1, tt, ATT_KV_WIDTH), lambda bi, j: (bi, j, kvb + 1)),
            pl.BlockSpec((tt, HEAD_DIM), lambda bi, j: (j, 0)),
            pl.BlockSpec((tt, HEAD_DIM), lambda bi, j: (j, 0)),
            pl.BlockSpec((1, HEAD_DIM), lambda bi, j: (0, 0)),
            pl.BlockSpec((1, HEAD_DIM), lambda bi, j: (0, 0)),
        ],
        out_specs=[
            pl.BlockSpec((1, tt, ATT_Q_WIDTH), lambda bi, j: (bi, j, 0)),
            pl.BlockSpec((1, tt, ATT_KV_WIDTH), lambda bi, j: (bi, j, 0)),
            pl.BlockSpec((1, tt, ATT_KV_WIDTH), lambda bi, j: (bi, j, 0)),
        ],
        out_shape=[
            jax.ShapeDtypeStruct((b, t, ATT_Q_WIDTH), BF16),
            jax.ShapeDtypeStruct((b, t, ATT_KV_WIDTH), BF16),
            jax.ShapeDtypeStruct((b, t, ATT_KV_WIDTH), BF16),
        ],
        compiler_params=_cparams(("parallel", "parallel")),
        name="att_prep",
    )(p, p, p, cos, sin_s, qg.reshape(1, HEAD_DIM), kg.reshape(1, HEAD_DIM))


ATT_ROW_BLOCK = 128
ATT_KEY_TILE = 256
ATT_KEY_CHUNK = 1024


def _attn_kernel(q_ref, k_ref, v_ref, o_ref, qs_ref, sa_ref, sb_ref, pa_ref, pb_ref, ala_ref, alb_ref,
                 m_ref, l_ref, acc_ref, *, tq, tk, n_ctx, lat_chunks, ctx_qtiles):
    i = pl.program_id(2)
    rows = ATT_GROUP * tq
    for g in range(ATT_GROUP):
        qs_ref[g * tq:(g + 1) * tq, :] = q_ref[0, :, g * HEAD_DIM:(g + 1) * HEAD_DIM]

    n_blocks = rows // ATT_ROW_BLOCK

    def max_pass(s_ref, al_ref, width, first, blocks):
        for r in blocks:
            rs = slice(r * ATT_ROW_BLOCK, (r + 1) * ATT_ROW_BLOCK)
            mx = s_ref[rs, 0:LANES]
            for c0 in range(LANES, width, LANES):
                mx = jnp.maximum(mx, s_ref[rs, c0:c0 + LANES])
            m_new = jnp.max(mx, axis=1, keepdims=True)
            if first:
                al_ref[rs, :] = jnp.ones_like(m_new)
            else:
                m_old = m_ref[rs, :]
                m_new = jnp.maximum(m_old, m_new)
                al_ref[rs, :] = jnp.exp2(m_old - m_new)
            m_ref[rs, :] = m_new

    def exp_pass(s_ref, p_ref, al_ref, width, first, blocks):
        for r in blocks:
            rs = slice(r * ATT_ROW_BLOCK, (r + 1) * ATT_ROW_BLOCK)
            m_new = m_ref[rs, :]
            psum = jnp.zeros((ATT_ROW_BLOCK, LANES), F32)
            for c0 in range(0, width, LANES):
                pr = jnp.exp2(s_ref[rs, c0:c0 + LANES] - m_new)
                psum = psum + pr
                p_ref[rs, c0:c0 + LANES] = pr.astype(p_ref.dtype)
            lsum = jnp.sum(psum, axis=1, keepdims=True)
            if first:
                l_ref[rs, :] = lsum
            else:
                l_ref[rs, :] = al_ref[rs, :] * l_ref[rs, :] + lsum

    def softmax_update(s_ref, p_ref, al_ref, width, first):
        max_pass(s_ref, al_ref, width, first, range(n_blocks))
        exp_pass(s_ref, p_ref, al_ref, width, first, range(n_blocks))

    def write_out():
        out = acc_ref[...] * (1.0 / l_ref[...])
        for g in range(ATT_GROUP):
            o_ref[0, :, g * HEAD_DIM:(g + 1) * HEAD_DIM] = out[g * tq:(g + 1) * tq].astype(o_ref.dtype)

    sb_ref[:, :n_ctx] = _dot_nt(qs_ref[...], k_ref[0, 0:n_ctx, :])

    @pl.when(i < ctx_qtiles)
    def _():
        softmax_update(sb_ref, pb_ref, alb_ref, n_ctx, True)
        acc_ref[...] = _dot(pb_ref[:, :n_ctx], v_ref[0, 0:n_ctx, :])
        write_out()

    @pl.when(i >= ctx_qtiles)
    def _():
        def rows_at(ref, start):
            if isinstance(start, int):
                return ref[0, start:start + tk, :]
            return ref[0, pl.ds(pl.multiple_of(start, LANES), tk), :]

        def k_chunk(c):
            return rows_at(k_ref, n_ctx + c * tk)

        def v_before(c):
            start = n_ctx + (c - 1) * tk
            return rows_at(v_ref, max(start, 0) if isinstance(start, int) else jnp.maximum(start, 0))

        def pv(p_ref, al_ref, vc):
            acc_ref[...] = al_ref[...] * acc_ref[...] + _dot(p_ref[...], vc)

        def stage(c, s_cur, p_cur, al_cur, s_nxt, p_prev, al_prev):
            pv(p_prev, al_prev, v_before(c))
            max_pass(s_cur, al_cur, tk, False, range(n_blocks))
            kt = min(ATT_KEY_TILE, tk)
            parts = tk // kt
            per = n_blocks // parts
            k_next = k_chunk(c + 1) if s_nxt is not None else None
            for j in range(parts):
                if s_nxt is not None:
                    ks = slice(j * kt, (j + 1) * kt)
                    s_nxt[:, ks] = _dot_nt(qs_ref[...], k_next[ks])
                exp_pass(s_cur, p_cur, al_cur, tk, False, range(j * per, (j + 1) * per))

        a_bufs = (sa_ref, pa_ref, ala_ref)
        b_bufs = (sb_ref, pb_ref, alb_ref)
        if n_ctx < tk:
            pb_ref[:, n_ctx:] = jnp.zeros((rows, tk - n_ctx), pb_ref.dtype)
        acc_ref[...] = jnp.zeros_like(acc_ref)
        sa_ref[...] = _dot_nt(qs_ref[...], k_chunk(0))
        softmax_update(sb_ref, pb_ref, alb_ref, n_ctx, True)

        def pair(cp, carry):
            c0 = 2 * cp
            stage(c0, *a_bufs, sb_ref, pb_ref, alb_ref)
            stage(c0 + 1, *b_bufs, sa_ref, pa_ref, ala_ref)
            return carry

        lax.fori_loop(0, lat_chunks // 2 - 1, pair, 0)
        stage(lat_chunks - 2, *a_bufs, sb_ref, pb_ref, alb_ref)
        stage(lat_chunks - 1, *b_bufs, None, pa_ref, ala_ref)
        pv(pb_ref, alb_ref, v_before(lat_chunks))
        write_out()


def _attention(qr, kr, vr, n_ctx):
    b, t, _ = qr.shape
    n_lat = t - n_ctx
    tq = _pick(n_ctx, 256, SUBLANES)
    tk = _pick(n_lat // 2, ATT_KEY_CHUNK, LANES)
    assert n_lat % (2 * tk) == 0 and n_ctx % LANES == 0 and n_ctx <= tk
    rows = ATT_GROUP * tq
    gw = ATT_GROUP * HEAD_DIM
    kern = functools.partial(_attn_kernel, tq=tq, tk=tk, n_ctx=n_ctx, lat_chunks=n_lat // tk,
                             ctx_qtiles=n_ctx // tq)
    return pl.pallas_call(
        kern,
        grid=(b, ATT_KV_HEADS, t // tq),
        in_specs=[
            pl.BlockSpec((1, tq, gw), lambda bi, h, i: (bi, i, h)),
            pl.BlockSpec((1, t, HEAD_DIM), lambda bi, h, i: (bi, 0, h)),
            pl.BlockSpec((1, t, HEAD_DIM), lambda bi, h, i: (bi, 0, h)),
        ],
        out_specs=pl.BlockSpec((1, tq, gw), lambda bi, h, i: (bi, i, h)),
        out_shape=jax.ShapeDtypeStruct((b, t, ATT_Q_WIDTH), BF16),
        scratch_shapes=[
            pltpu.VMEM((rows, HEAD_DIM), BF16),
            pltpu.VMEM((rows, tk), F32), pltpu.VMEM((rows, tk), F32),
            pltpu.VMEM((rows, tk), BF16), pltpu.VMEM((rows, tk), BF16),
            pltpu.VMEM((rows, 1), F32), pltpu.VMEM((rows, 1), F32),
            pltpu.VMEM((rows, 1), F32), pltpu.VMEM((rows, 1), F32),
            pltpu.VMEM((rows, HEAD_DIM), F32),
        ],
        compiler_params=_cparams(("parallel", "parallel", "parallel")),
        name="attention",
    )(qr, kr, vr)


def _chunk_maps(n_ctx, t):
    ncc = n_ctx // CHUNK
    nc = t // CHUNK

    def fwd(i):
        return i

    def bwd(i):
        return jnp.where(i < ncc, ncc - 1 - i, nc - 1 + ncc - i)

    return fwd, bwd, nc


def _gla_chunk(q_raw, f_raw, v, lb, cm, lv, s_ref):
    c = CHUNK
    q = _silu(q_raw)
    one_m_lb = 1.0 - lb
    f_gate = lb + one_m_lb * _sigmoid(f_raw)
    log_f = jnp.log(jnp.maximum(f_gate, GATE_FLOOR))
    key = one_m_lb * _sigmoid(-f_raw)
    hi = log_f.astype(BF16)
    r1 = log_f - hi.astype(F32)
    mid = r1.astype(BF16)
    lo = (r1 - mid.astype(F32)).astype(BF16)
    x = _dot(cm, hi) + _dot(cm, mid) + _dot(cm, lo)
    cum = x[0:c]
    tot = x[7 * c:7 * c + 1]
    att = jnp.where(lv[6] > 0, _dot_nt(q.astype(BF16), key.astype(BF16)), 0.0)
    for n in range(len(_LEVELS)):
        ref = x[(n + 1) * c:(n + 2) * c]
        qs = (q * jnp.exp(jnp.minimum(cum - ref, 0.0))).astype(BF16)
        ks = (key * jnp.exp(jnp.minimum(ref - cum, 0.0))).astype(BF16)
        att = att + jnp.where(lv[n] > 0, _dot_nt(qs, ks), 0.0)
    st = s_ref[...]
    vb = v.astype(BF16)
    o = _dot_nt((q * jnp.exp(cum)).astype(BF16), st.astype(BF16)) + _dot(att.astype(BF16), vb)
    k_dec = (key * jnp.exp(tot - cum)).astype(BF16)
    s_ref[...] = st * jnp.exp(tot) + _dot_tn(vb, k_dec)
    return o


def _hg_kernel(qf_ref, ff_ref, if_ref, qb_ref, fb_ref, ib_ref, lbf_ref, lbb_ref, cm_ref, lv_ref,
               of_ref, ob_ref, s_ref):
    @pl.when(pl.program_id(2) == 0)
    def _():
        s_ref[...] = jnp.zeros_like(s_ref)

    of_ref[0] = _gla_chunk(qf_ref[0], ff_ref[0], if_ref[0], lbf_ref[0], cm_ref[0], lv_ref[0], s_ref.at[0])
    ob_ref[0] = _gla_chunk(qb_ref[0], fb_ref[0], ib_ref[0], lbb_ref[0], cm_ref[1], lv_ref[1], s_ref.at[1])


def _hgrn2(p, lower_bound, n_ctx):
    b, t, _ = p.shape
    fwd, bwd, nc = _chunk_maps(n_ctx, t)
    cm, lv, _, _ = _dir_consts()
    cm = jnp.asarray(cm, BF16)
    lv = jnp.asarray(lv, F32)
    lb = lower_bound.astype(F32).reshape(2 * HG_HEADS, 1, HEAD_DIM)
    hq, hf, hi_ = C_HQ // HEAD_DIM, C_HF // HEAD_DIM, C_HI // HEAD_DIM
    blk = (1, CHUNK, HEAD_DIM)

    def col(base, order):
        return pl.BlockSpec(blk, lambda bi, h, i: (bi, order(i), base + h))

    full = lambda shape: pl.BlockSpec(shape, lambda bi, h, i: (0,) * len(shape))
    return pl.pallas_call(
        _hg_kernel,
        grid=(b, HG_HEADS, nc),
        in_specs=[
            col(hq, fwd), col(hf, fwd), col(hi_, fwd),
            col(hq, bwd), col(hf + HG_HEADS, bwd), col(hi_, bwd),
            pl.BlockSpec((1, 1, HEAD_DIM), lambda bi, h, i: (h, 0, 0)),
            pl.BlockSpec((1, 1, HEAD_DIM), lambda bi, h, i: (HG_HEADS + h, 0, 0)),
            full(cm.shape), full(lv.shape),
        ],
        out_specs=[
            pl.BlockSpec(blk, lambda bi, h, i: (bi, fwd(i), h)),
            pl.BlockSpec(blk, lambda bi, h, i: (bi, bwd(i), h)),
        ],
        out_shape=[jax.ShapeDtypeStruct((b, t, HG_WIDTH), F32)] * 2,
        scratch_shapes=[pltpu.VMEM((2, HEAD_DIM, HEAD_DIM), F32)],
        compiler_params=_cparams(("parallel", "parallel", "arbitrary")),
        name="hgrn2",
    )(p, p, p, p, p, p, lb, lb, cm, lv)


def _dn_prep_kernel(x_ref, prev_ref, next_ref, ab_ref, w_ref, alog_ref, dt_ref, cs_ref,
                    q_ref, k_ref, v_ref, gcol_ref, gtot_ref, gct_ref, *, tt, ctx_tiles, n_tiles):
    j = pl.program_id(1)
    seg_start = jnp.logical_or(j == 0, j == ctx_tiles)
    seg_end = jnp.logical_or(j == ctx_tiles - 1, j == n_tiles - 1)
    x = x_ref[0]
    ph = jnp.where(seg_start, 0.0, prev_ref[0])
    nh = jnp.where(seg_end, 0.0, next_ref[0])
    w = w_ref[0]
    r8 = lax.broadcasted_iota(jnp.int32, (SUBLANES, x.shape[1]), 0)
    half = DN_CONV // 2

    def later(s):
        r = pltpu.roll(x, tt - s, 0)
        tail = jnp.where(r8 >= SUBLANES - s, pltpu.roll(nh, SUBLANES - s, 0), r[tt - SUBLANES:])
        return jnp.concatenate([r[:tt - SUBLANES], tail], axis=0)

    def earlier(s):
        r = pltpu.roll(x, s, 0)
        head = jnp.where(r8 < s, pltpu.roll(ph, s, 0), r[:SUBLANES])
        return jnp.concatenate([head, r[SUBLANES:]], axis=0)

    acc = x * w[half:half + 1]
    for s in range(1, half + 1):
        acc = acc + later(s) * w[half + s:half + s + 1] + earlier(s) * w[half - s:half - s + 1]
    y = _silu(acc)
    for h in range(DN_HEADS):
        qh = y[:, h * HEAD_DIM:(h + 1) * HEAD_DIM]
        kh = y[:, DN_WIDTH + h * HEAD_DIM:DN_WIDTH + (h + 1) * HEAD_DIM]
        q_ref[0, h] = qh * lax.rsqrt(jnp.sum(qh * qh, axis=-1, keepdims=True) + EPS) * HEAD_DIM ** -0.5
        k_ref[0, h] = kh * lax.rsqrt(jnp.sum(kh * kh, axis=-1, keepdims=True) + EPS)
        v_ref[0, h] = y[:, 2 * DN_WIDTH + h * HEAD_DIM:2 * DN_WIDTH + (h + 1) * HEAD_DIM]
    a = ab_ref[0]
    z = a + dt_ref[...]
    softplus = jnp.maximum(z, 0.0) + jnp.log1p(jnp.exp(-jnp.abs(z)))
    lane = lax.broadcasted_iota(jnp.int32, a.shape, 1)
    gb = jnp.where(lane < 2 * DN_HEADS, -jnp.exp(alog_ref[...]) * softplus, _sigmoid(a))
    hi = gb.astype(BF16)
    r1 = gb - hi.astype(F32)
    mid = r1.astype(BF16)
    lo = (r1 - mid.astype(F32)).astype(BF16)
    cmat = cs_ref[...]
    xs = _dot(cmat, hi) + _dot(cmat, mid) + _dot(cmat, lo)
    gcol = jnp.where(lane < DN_HEADS, xs[:tt], jnp.where(lane < 2 * DN_HEADS, xs[tt:2 * tt], gb))
    gcol_ref[0] = gcol
    gtot_ref[0] = xs[2 * tt:]
    gct_ref[0] = gcol.T


def _dn_prep(p, conv_w, a_log, dt_bias, n_ctx):
    b, t, _ = p.shape
    tt = _pick(n_ctx, 128, LANES)
    n_tiles = t // tt
    r = tt // SUBLANES
    last8 = t // SUBLANES - 1
    cw = 3 * DN_WIDTH
    cb = C_DQ // cw
    pad = lambda v: jnp.pad(v.astype(F32).reshape(1, -1), ((0, 0), (0, LANES - 2 * DN_HEADS)))
    kern = functools.partial(_dn_prep_kernel, tt=tt, ctx_tiles=n_ctx // tt, n_tiles=n_tiles)
    hshape = jax.ShapeDtypeStruct((b, DN_HEADS, t, HEAD_DIM), F32)
    hspec = pl.BlockSpec((1, DN_HEADS, tt, HEAD_DIM), lambda bi, j: (bi, 0, j, 0))
    _, _, incl, _ = _dir_consts()
    eye = np.eye(tt // CHUNK, dtype=np.float32)
    cs = np.concatenate([np.kron(eye, incl[0]), np.kron(eye, incl[1]),
                         np.kron(eye, np.ones((CHUNK, CHUNK), np.float32))], axis=0)
    tile = pl.BlockSpec((1, tt, LANES), lambda bi, j: (bi, j, 0))
    q, k, v, gcol, gtot, gct = pl.pallas_call(
        kern,
        grid=(b, n_tiles),
        in_specs=[
            pl.BlockSpec((1, tt, cw), lambda bi, j: (bi, j, cb)),
            pl.BlockSpec((1, SUBLANES, cw), lambda bi, j: (bi, jnp.maximum(j * r - 1, 0), cb)),
            pl.BlockSpec((1, SUBLANES, cw), lambda bi, j: (bi, jnp.minimum((j + 1) * r, last8), cb)),
            pl.BlockSpec((1, tt, LANES), lambda bi, j: (bi, j, C_AB // LANES)),
            pl.BlockSpec((1, DN_CONV, cw), lambda bi, j: (0, 0, 0)),
            pl.BlockSpec((1, LANES), lambda bi, j: (0, 0)),
            pl.BlockSpec((1, LANES), lambda bi, j: (0, 0)),
            pl.BlockSpec((3 * tt, tt), lambda bi, j: (0, 0)),
        ],
        out_specs=[hspec, hspec, hspec, tile, tile,
                   pl.BlockSpec((1, LANES, tt), lambda bi, j: (bi, 0, j))],
        out_shape=[hshape, hshape, hshape,
                   jax.ShapeDtypeStruct((b, t, LANES), F32),
                   jax.ShapeDtypeStruct((b, t, LANES), F32),
                   jax.ShapeDtypeStruct((b, LANES, t), F32)],
        compiler_params=_cparams(("parallel", "parallel")),
        name="dn_prep",
    )(p, p, p, p, conv_w.astype(F32)[None], pad(a_log), pad(dt_bias), jnp.asarray(cs, BF16))
    gcr = gct[:, :2 * DN_HEADS].reshape(b, 2 * DN_HEADS, t // CHUNK, CHUNK).transpose(0, 2, 1, 3)
    return q, k, v, gcol, gtot, gcr


def _bdot(a, b):
    return lax.dot_general(a, b, (((2,), (1,)), ((0,), (0,))), preferred_element_type=F32)


def _bdot_nt(a, b):
    return lax.dot_general(a, b, (((2,), (2,)), ((0,), (0,))), preferred_element_type=F32)


def _bdot3(a, b):
    ah, al = _split2(a)
    bh, bl = _split2(b)
    return _bdot(ah, bh) + _bdot(ah, bl) + _bdot(al, bh)


def _tri_inverse(a, eye, bd16, off32, off64):
    n = -(a * bd16)
    s = eye + n
    p = n
    for _ in range(3):
        p = _bdot3(p, p)
        s = s + _bdot3(p, s)
    for off in (off32, off64):
        s = s - _bdot3(_bdot3(s, a * off), s)
    return s


def _dn_intra_kernel(q_ref, k_ref, v_ref, gcol_ref, gtot_ref, gcr_ref, incl_ref, off32_ref, off64_ref, bd_ref,
                     u_ref, w_ref, qd_ref, kd_ref, qk_ref, *, nch):
    h = pl.program_id(1)
    c = CHUNK
    tb = nch * c
    split = lambda a: a.reshape(nch, c, a.shape[-1])
    both = lambda a: jnp.concatenate([a, a], axis=0)
    q3, k3, v3 = split(q_ref[0, 0]), split(k_ref[0, 0]), split(v_ref[0, 0])
    gcol = gcol_ref[0]
    gtot = gtot_ref[0]
    lane = lax.broadcasted_iota(jnp.int32, gcol.shape, 1)
    pick = lambda a, j: split(jnp.sum(jnp.where(lane == j, a, 0.0), axis=1, keepdims=True))
    js = [d * DN_HEADS + h for d in range(2)]
    gc_c = jnp.concatenate([pick(gcol, j) for j in js], axis=0)
    beta = jnp.concatenate([pick(gcol, 2 * DN_HEADS + j) for j in js], axis=0)
    tot = jnp.concatenate([pick(gtot, j) for j in js], axis=0)
    gc_r = jnp.concatenate([gcr_ref[0, :, pl.ds(j, 1), :] for j in js], axis=0)
    q2, k2, v2 = both(q3), both(k3), both(v3)
    incl = incl_ref[...]
    eye = (lax.broadcasted_iota(jnp.int32, (c, c), 0) == lax.broadcasted_iota(jnp.int32, (c, c), 1)).astype(F32)
    decay = jnp.where(incl > 0, jnp.exp(jnp.minimum(gc_c - gc_r, 0.0)), 0.0)
    kb = k2 * beta
    k2b = k2.astype(BF16)
    a = (incl - eye) * _bdot_nt(kb.astype(BF16), k2b) * decay
    t_inv = _tri_inverse(a, eye, bd_ref[...], off32_ref[...], off64_ref[...]).astype(BF16)
    e_gc = jnp.exp(gc_c)
    u = _bdot(t_inv, (v2 * beta).astype(BF16))
    w = _bdot(t_inv, (kb * e_gc).astype(BF16))
    qd = q2 * e_gc
    kd = k2 * jnp.exp(tot - gc_c)
    qk = both(_bdot_nt(q3.astype(BF16), k3.astype(BF16))) * decay
    for d in range(2):
        sl = slice(d * nch, (d + 1) * nch)
        u_ref[d, 0, 0] = u[sl].reshape(tb, HEAD_DIM)
        w_ref[d, 0, 0] = w[sl].reshape(tb, HEAD_DIM).astype(w_ref.dtype)
        qd_ref[d, 0, 0] = qd[sl].reshape(tb, HEAD_DIM).astype(qd_ref.dtype)
        kd_ref[d, 0, 0] = kd[sl].reshape(tb, HEAD_DIM).astype(kd_ref.dtype)
        qk_ref[d, 0, 0] = qk[sl].reshape(tb, c).astype(qk_ref.dtype)


def _dn_intra(q, k, v, gcol, gtot, gcr):
    b, nh, t, _ = q.shape
    tb = _pick(t, 256, CHUNK)
    nch = tb // CHUNK
    _, lv, incl, bd16 = _dir_consts()
    rep = lambda m: jnp.asarray(np.repeat(m, nch, axis=0))
    incl2, off32, off64 = rep(incl), rep(lv[:, 1]), rep(lv[:, 0])
    bd16 = jnp.asarray(bd16)
    hspec = pl.BlockSpec((1, 1, tb, HEAD_DIM), lambda bi, h, i: (bi, h, i, 0))
    tile = pl.BlockSpec((1, tb, LANES), lambda bi, h, i: (bi, i, 0))
    full = lambda a: pl.BlockSpec(a.shape, lambda bi, h, i: (0,) * a.ndim)
    ospec = pl.BlockSpec((2, 1, 1, tb, HEAD_DIM), lambda bi, h, i: (0, bi, h, i, 0))
    oshape = lambda dt: jax.ShapeDtypeStruct((2, b, nh, t, HEAD_DIM), dt)
    return pl.pallas_call(
        functools.partial(_dn_intra_kernel, nch=nch),
        grid=(b, nh, t // tb),
        in_specs=[
            hspec, hspec, hspec, tile, tile,
            pl.BlockSpec((1, nch, 2 * DN_HEADS, CHUNK), lambda bi, h, i: (bi, i, 0, 0)),
            full(incl2), full(off32), full(off64), full(bd16),
        ],
        out_specs=[ospec, ospec, ospec, ospec,
                   pl.BlockSpec((2, 1, 1, tb, CHUNK), lambda bi, h, i: (0, bi, h, i, 0))],
        out_shape=[oshape(F32), oshape(BF16), oshape(BF16), oshape(BF16),
                   jax.ShapeDtypeStruct((2, b, nh, t, CHUNK), BF16)],
        compiler_params=_cparams(("parallel", "parallel", "parallel")),
        name="dn_intra",
    )(q, k, v, gcol, gtot, gcr, incl2, off32, off64, bd16)


def _dn_scan_kernel(*refs):
    ins = refs[:12]
    of_ref, ob_ref, s_ref = refs[12:]

    @pl.when(pl.program_id(1) == 0)
    def _():
        s_ref[...] = jnp.zeros_like(s_ref)

    for d, o_ref in ((0, of_ref), (1, ob_ref)):
        u_ref, w_ref, qd_ref, kd_ref, qk_ref, gtot_ref = ins[6 * d:6 * d + 6]
        last = jnp.exp(gtot_ref[0, 0:1, :])
        for h in range(DN_HEADS):
            s = s_ref[d, h]
            wq = jnp.concatenate([w_ref[0, 0, h], qd_ref[0, 0, h]], axis=0)
            r = _dot(wq, s.astype(BF16))
            v_new = (u_ref[0, 0, h] - r[:CHUNK]).astype(BF16)
            o_ref[0, :, h * HEAD_DIM:(h + 1) * HEAD_DIM] = r[CHUNK:] + _dot(qk_ref[0, 0, h], v_new)
            j = d * DN_HEADS + h
            s_ref[d, h] = s * last[:, j:j + 1] + _dot_tn(kd_ref[0, 0, h], v_new)


def _dn_scan(u, w, qd, kd, qk, gtot, n_ctx):
    _, b, nh, t, _ = u.shape
    fwd, bwd, nc = _chunk_maps(n_ctx, t)

    def specs(d, order):
        hs = lambda width: pl.BlockSpec((1, 1, nh, CHUNK, width), lambda bi, i: (d, bi, 0, order(i), 0))
        return [hs(HEAD_DIM)] * 4 + [hs(CHUNK), pl.BlockSpec((1, CHUNK, LANES), lambda bi, i: (bi, order(i), 0))]

    args = (u, w, qd, kd, qk, gtot)
    return pl.pallas_call(
        _dn_scan_kernel,
        grid=(b, nc),
        in_specs=specs(0, fwd) + specs(1, bwd),
        out_specs=[
            pl.BlockSpec((1, CHUNK, DN_WIDTH), lambda bi, i: (bi, fwd(i), 0)),
            pl.BlockSpec((1, CHUNK, DN_WIDTH), lambda bi, i: (bi, bwd(i), 0)),
        ],
        out_shape=[jax.ShapeDtypeStruct((b, t, DN_WIDTH), F32)] * 2,
        scratch_shapes=[pltpu.VMEM((2, nh, HEAD_DIM, HEAD_DIM), F32)],
        compiler_params=_cparams(("parallel", "arbitrary")),
        name="dn_scan",
    )(*args, *args)


def _mix_finish_kernel(of_ref, ob_ref, z_ref, g_ref, o_ref):
    g = g_ref[...]
    for h in range(of_ref.shape[2] // HEAD_DIM):
        sl = slice(h * HEAD_DIM, (h + 1) * HEAD_DIM)
        o = of_ref[0, :, sl] + ob_ref[0, :, sl]
        y = o * lax.rsqrt(jnp.mean(o * o, axis=-1, keepdims=True) + EPS) * g
        o_ref[0, :, sl] = (y * _silu(z_ref[0, :, sl])).astype(o_ref.dtype)


def _mix_finish(o_f, o_b, p, gate_col, gain):
    b, t, width = o_f.shape
    tt = _pick(t, 256, SUBLANES)
    spec = pl.BlockSpec((1, tt, width), lambda bi, j: (bi, j, 0))
    return pl.pallas_call(
        _mix_finish_kernel,
        grid=(b, t // tt),
        in_specs=[spec, spec,
                  pl.BlockSpec((1, tt, width), lambda bi, j: (bi, j, gate_col // width)),
                  pl.BlockSpec((1, HEAD_DIM), lambda bi, j: (0, 0))],
        out_specs=spec,
        out_shape=jax.ShapeDtypeStruct((b, t, width), BF16),
        compiler_params=_cparams(("parallel", "parallel")),
        name="mix_finish",
    )(o_f, o_b, p, gain.astype(F32).reshape(1, HEAD_DIM))


def _out_proj_kernel(a1_ref, a2_ref, a3_ref, w1_ref, w2_ref, w3_ref, x_ref, g_ref, o_ref):
    y = _dot(a1_ref[0], w1_ref[...]) + _dot(a2_ref[0], w2_ref[...]) + _dot(a3_ref[0], w3_ref[...])
    o_ref[0] = x_ref[0] + g_ref[0] * y


def _out_proj(oa, ob, og, w_out, x, mod_rows, layer, n_ctx):
    b, t, d = x.shape
    tm = _pick(n_ctx, 256, SUBLANES)
    tn = _pick(d, 1024, LANES)
    nct = n_ctx // tm
    w = w_out.astype(BF16)
    return pl.pallas_call(
        _out_proj_kernel,
        grid=(d // tn, b, t // tm),
        in_specs=[
            pl.BlockSpec((1, tm, DN_WIDTH), lambda n, bi, i: (bi, i, 0)),
            pl.BlockSpec((1, tm, HG_WIDTH), lambda n, bi, i: (bi, i, 0)),
            pl.BlockSpec((1, tm, ATT_Q_WIDTH), lambda n, bi, i: (bi, i, 0)),
            pl.BlockSpec((DN_WIDTH, tn), lambda n, bi, i: (0, n)),
            pl.BlockSpec((HG_WIDTH, tn), lambda n, bi, i: (1, n)),
            pl.BlockSpec((ATT_Q_WIDTH, tn), lambda n, bi, i: (1, n)),
            pl.BlockSpec((1, tm, tn), lambda n, bi, i: (bi, i, n)),
            pl.BlockSpec((1, 1, tn), lambda n, bi, i: (
                (layer * SUBLANES + jnp.where(i < nct, b, bi)) * N_MOD + 2, 0, n)),
        ],
        out_specs=pl.BlockSpec((1, tm, tn), lambda n, bi, i: (bi, i, n)),
        out_shape=jax.ShapeDtypeStruct((b, t, d), F32),
        compiler_params=_cparams(("parallel", "parallel", "parallel")),
        name="out_proj",
    )(oa, ob, og, w, w, w, x, mod_rows)


def _router_kernel(h_ref, w_ref, b_ref, idx_ref, wt_ref, rank_ref, cnt_ref, carry_ref):
    i = pl.program_id(0)

    @pl.when(i == 0)
    def _():
        carry_ref[...] = jnp.zeros_like(carry_ref)

    tm = h_ref.shape[0]
    neg = -jnp.inf
    logits = _dot(h_ref[...].astype(BF16), w_ref[...])
    scores = _sigmoid(logits)
    lane = lax.broadcasted_iota(jnp.int32, scores.shape, 1)
    valid = lane < N_EXPERTS
    grp = lane // GROUP_SIZE
    big = jnp.int32(1 << 20)
    sel = jnp.where(valid, scores + b_ref[...], neg)
    rmax = lambda a: jnp.max(a, axis=1, keepdims=True)
    rmin = lambda a: jnp.min(a, axis=1, keepdims=True)

    grp_score = jnp.full(scores.shape, neg, F32)
    for g in range(N_GROUPS):
        in_g = grp == g
        cur = jnp.where(in_g, sel, neg)
        m1 = rmax(cur)
        i1 = rmin(jnp.where(cur == m1, lane, big))
        m2 = rmax(jnp.where(lane == i1, neg, cur))
        grp_score = jnp.where(in_g, m1 + m2, grp_score)
    keep = jnp.zeros(scores.shape, jnp.bool_)
    cur = grp_score
    for _ in range(TOPK_GROUPS):
        m = rmax(cur)
        gsel = rmin(jnp.where(cur == m, grp, big))
        hit = grp == gsel
        keep = jnp.logical_or(keep, hit)
        cur = jnp.where(hit, neg, cur)
    cur = jnp.where(keep, sel, neg)
    idx_out = jnp.zeros(scores.shape, jnp.int32)
    w_out = jnp.zeros(scores.shape, F32)
    onehots = []
    for k in range(TOP_K):
        m = rmax(cur)
        ik = rmin(jnp.where(cur == m, lane, big))
        hit = lane == ik
        sk = jnp.sum(jnp.where(hit, scores, 0.0), axis=1, keepdims=True)
        idx_out = jnp.where(lane == k, ik, idx_out)
        w_out = jnp.where(lane == k, sk, w_out)
        onehots.append(hit)
        cur = jnp.where(hit, neg, cur)
    w_out = w_out * (ROUTED_SCALE / jnp.sum(w_out, axis=1, keepdims=True))
    chosen = onehots[0]
    for oh in onehots[1:]:
        chosen = jnp.logical_or(chosen, oh)
    e_mat = jnp.where(chosen, 1.0, 0.0)
    r_i = lax.broadcasted_iota(jnp.int32, (tm, tm), 0)
    c_i = lax.broadcasted_iota(jnp.int32, (tm, tm), 1)
    lower = jnp.where(c_i < r_i, 1.0, 0.0).astype(BF16)
    before = _dot(lower, e_mat.astype(BF16)) + carry_ref[...]
    rank_out = jnp.zeros(scores.shape, jnp.int32)
    for k in range(TOP_K):
        rk = jnp.sum(jnp.where(onehots[k], before, 0.0), axis=1, keepdims=True)
        rank_out = jnp.where(lane == k, rk.astype(jnp.int32), rank_out)
    carry_ref[...] = carry_ref[...] + jnp.sum(e_mat, axis=0, keepdims=True)
    idx_ref[...] = idx_out
    wt_ref[...] = w_out
    rank_ref[...] = rank_out
    cnt_ref[...] = jnp.broadcast_to(carry_ref[...], cnt_ref.shape).astype(jnp.int32)


def _router(h, w_router, bias):
    n, d = h.shape
    tm = _pick(n, 256, SUBLANES)
    w = jnp.pad(w_router, ((0, 0), (0, LANES - N_EXPERTS))).astype(BF16)
    bz = jnp.pad(bias.astype(F32).reshape(1, -1), ((0, 0), (0, LANES - N_EXPERTS)))
    tile = pl.BlockSpec((tm, LANES), lambda i: (i, 0))
    return pl.pallas_call(
        _router_kernel,
        grid=(n // tm,),
        in_specs=[
            pl.BlockSpec((tm, d), lambda i: (i, 0)),
            pl.BlockSpec((d, LANES), lambda i: (0, 0)),
            pl.BlockSpec((1, LANES), lambda i: (0, 0)),
        ],
        out_specs=[tile, tile, tile, pl.BlockSpec((SUBLANES, LANES), lambda i: (0, 0))],
        out_shape=[
            jax.ShapeDtypeStruct((n, LANES), jnp.int32),
            jax.ShapeDtypeStruct((n, LANES), F32),
            jax.ShapeDtypeStruct((n, LANES), jnp.int32),
            jax.ShapeDtypeStruct((SUBLANES, LANES), jnp.int32),
        ],
        scratch_shapes=[pltpu.VMEM((1, LANES), F32)],
        compiler_params=_cparams(("arbitrary",)),
        name="router",
    )(h, w, bz)


def _expert_kernel(be_ref, nu_ref, tok_ref, nxt_ref, h_hbm, wg_ref, wu_ref, wd_ref, y_ref, xbuf, sem):
    i = pl.program_id(0)
    n_used = nu_ref[0]
    slot = i % 2

    def row_copy(tok, slot_, r):
        return pltpu.make_async_copy(h_hbm.at[pl.ds(tok, 1)], xbuf.at[slot_, pl.ds(r, 1)], sem.at[slot_])

    def start_gather(idx_ref, slot_):
        def body(r, c):
            row_copy(idx_ref[0, 0, r], slot_, r).start()
            return c
        lax.fori_loop(0, MOE_ROWS, body, 0, unroll=DMA_UNROLL)

    @pl.when(jnp.logical_and(i == 0, n_used > 0))
    def _():
        start_gather(tok_ref, 0)

    @pl.when(i + 1 < n_used)
    def _():
        start_gather(nxt_ref, 1 - slot)

    @pl.when(i < n_used)
    def _():
        pltpu.make_async_copy(h_hbm.at[pl.ds(0, MOE_ROWS)], xbuf.at[slot], sem.at[slot]).wait()
        x = xbuf[slot].astype(BF16)
        a = _silu(_dot(x, wg_ref[0])) * _dot(x, wu_ref[0])
        y_ref[...] = _dot(a.astype(BF16), wd_ref[0])

    @pl.when(i >= n_used)
    def _():
        y_ref[...] = jnp.zeros_like(y_ref)


def _routed_experts(h, row_tok, block_e, n_used, wg, wu, wd):
    n, d = h.shape
    n_blocks = block_e.shape[0]
    ff = wg.shape[2]
    tok3 = row_tok.reshape(n_blocks, 1, MOE_ROWS)
    grid_spec = pltpu.PrefetchScalarGridSpec(
        num_scalar_prefetch=2,
        grid=(n_blocks,),
        in_specs=[
            pl.BlockSpec((1, 1, MOE_ROWS), lambda i, be, nu: (i, 0, 0), memory_space=pltpu.SMEM),
            pl.BlockSpec((1, 1, MOE_ROWS), lambda i, be, nu: (jnp.minimum(i + 1, n_blocks - 1), 0, 0),
                         memory_space=pltpu.SMEM),
            pl.BlockSpec(memory_space=pl.ANY),
            pl.BlockSpec((1, d, ff), lambda i, be, nu: (be[i], 0, 0)),
            pl.BlockSpec((1, d, ff), lambda i, be, nu: (be[i], 0, 0)),
            pl.BlockSpec((1, ff, d), lambda i, be, nu: (be[i], 0, 0)),
        ],
        out_specs=pl.BlockSpec((MOE_ROWS, d), lambda i, be, nu: (i, 0)),
        scratch_shapes=[pltpu.VMEM((2, MOE_ROWS, d), F32), pltpu.SemaphoreType.DMA((2,))],
    )
    return pl.pallas_call(
        _expert_kernel,
        grid_spec=grid_spec,
        out_shape=jax.ShapeDtypeStruct((n_blocks * MOE_ROWS, d), F32),
        compiler_params=_cparams(("arbitrary",)),
        name="routed_experts",
    )(block_e, n_used, tok3, tok3, h, wg, wu, wd)


def _shared_expert_kernel(h_ref, wsg_ref, wsu_ref, wsd_ref, o_ref):
    hb = h_ref[...].astype(BF16)
    a = _silu(_dot(hb, wsg_ref[...])) * _dot(hb, wsu_ref[...])
    o_ref[...] = _dot(a.astype(BF16), wsd_ref[...])


def _shared_expert(h, wsg, wsu, wsd):
    n, d = h.shape
    ff = wsg.shape[1]
    tm = _pick(n, 512, SUBLANES)
    tile = pl.BlockSpec((tm, d), lambda i: (i, 0))
    return pl.pallas_call(
        _shared_expert_kernel,
        grid=(n // tm,),
        in_specs=[tile,
                  pl.BlockSpec((d, ff), lambda i: (0, 0)),
                  pl.BlockSpec((d, ff), lambda i: (0, 0)),
                  pl.BlockSpec((ff, d), lambda i: (0, 0))],
        out_specs=tile,
        out_shape=jax.ShapeDtypeStruct((n, d), F32),
        compiler_params=_cparams(("parallel",)),
        name="shared_expert",
    )(h, wsg, wsu, wsd)


def _combine_kernel(pos_ref, nxt_ref, y_hbm, wt_ref, sh_ref, x_ref, g_ref, o_ref, ybuf, sem, *, tt):
    i = pl.program_id(0)
    n = pl.num_programs(0)
    slot = i % 2
    rows = tt * TOP_K

    def row_copy(src, slot_, r):
        return pltpu.make_async_copy(y_hbm.at[pl.ds(src, 1)], ybuf.at[slot_, pl.ds(r, 1)], sem.at[slot_])

    def start_gather(idx_ref, slot_):
        def body(r, c):
            row_copy(idx_ref[0, 0, r], slot_, r).start()
            return c
        lax.fori_loop(0, rows, body, 0, unroll=DMA_UNROLL)

    @pl.when(i == 0)
    def _():
        start_gather(pos_ref, 0)

    @pl.when(i + 1 < n)
    def _():
        start_gather(nxt_ref, 1 - slot)

    pltpu.make_async_copy(y_hbm.at[pl.ds(0, rows)], ybuf.at[slot], sem.at[slot]).wait()
    wt = wt_ref[...]
    cw = min(COMBINE_COLS, x_ref.shape[1])
    for c0 in range(0, x_ref.shape[1], cw):
        cs = slice(c0, c0 + cw)
        acc = sh_ref[:, cs]
        for k in range(TOP_K):
            acc = acc + wt[:, k:k + 1] * ybuf[slot, k * tt:(k + 1) * tt, cs]
        o_ref[:, cs] = x_ref[:, cs] + g_ref[0, :, cs] * acc


def _combine(y, pos, wts, shared, x, mod_rows, layer, batch, n_ctx):
    n, d = x.shape
    t = n // batch
    tt = _pick(n_ctx, 64, SUBLANES)
    n_tiles = n // tt
    tiles_per_b = t // tt
    nct = n_ctx // tt
    pos3 = pos.reshape(n_tiles, tt, TOP_K).transpose(0, 2, 1).reshape(n_tiles, 1, tt * TOP_K)
    row = lambda i: (layer * SUBLANES + jnp.where(i % tiles_per_b < nct, batch, i // tiles_per_b)) * N_MOD + 5
    kern = functools.partial(_combine_kernel, tt=tt)
    tile = pl.BlockSpec((tt, d), lambda i: (i, 0))
    return pl.pallas_call(
        kern,
        grid=(n_tiles,),
        in_specs=[
            pl.BlockSpec((1, 1, tt * TOP_K), lambda i: (i, 0, 0), memory_space=pltpu.SMEM),
            pl.BlockSpec((1, 1, tt * TOP_K), lambda i: (jnp.minimum(i + 1, n_tiles - 1), 0, 0),
                         memory_space=pltpu.SMEM),
            pl.BlockSpec(memory_space=pl.ANY),
            pl.BlockSpec((tt, LANES), lambda i: (i, 0)),
            tile, tile,
            pl.BlockSpec((1, 1, d), lambda i: (row(i), 0, 0)),
        ],
        out_specs=tile,
        out_shape=jax.ShapeDtypeStruct((n, d), F32),
        scratch_shapes=[pltpu.VMEM((2, tt * TOP_K, d), F32), pltpu.SemaphoreType.DMA((2,))],
        compiler_params=_cparams(("arbitrary",)),
        name="moe_combine",
    )(pos3, pos3, y, wts, shared, x, mod_rows)


def _moe(h, x, mod_rows, layer, w_router, router_bias, wg, wu, wd, wsg, wsu, wsd, batch, n_ctx):
    n, d = h.shape
    idx, wts, rank, counts = _router(h, w_router, router_bias)
    idx = idx[:, :TOP_K]
    rank = rank[:, :TOP_K]
    counts = counts[0, :N_EXPERTS]
    padded = (counts + MOE_ROWS - 1) // MOE_ROWS * MOE_ROWS
    pad_end = jnp.cumsum(padded)
    pad_start = pad_end - padded
    pos = pad_start[idx] + rank
    n_blocks = -(-(n * TOP_K) // MOE_ROWS) + N_EXPERTS
    tok = jnp.broadcast_to(jnp.arange(n, dtype=jnp.int32)[:, None], pos.shape)
    row_tok = jnp.zeros((n_blocks * MOE_ROWS,), jnp.int32).at[pos.reshape(-1)].set(tok.reshape(-1))
    block_start = jnp.arange(n_blocks, dtype=jnp.int32) * MOE_ROWS
    block_e = jnp.minimum(jnp.sum(pad_end[None, :] <= block_start[:, None], axis=1), N_EXPERTS - 1).astype(jnp.int32)
    n_used = (pad_end[-1] // MOE_ROWS).astype(jnp.int32).reshape(1)
    y = _routed_experts(h, row_tok, block_e, n_used, wg, wu, wd)
    shared = _shared_expert(h, wsg, wsu, wsd)
    return _combine(y, pos.astype(jnp.int32), wts, shared, x, mod_rows, layer, batch, n_ctx)


def _rope_tables(n_ctx, n_lat):
    rows = n_lat // GRID_W
    row, col = jnp.meshgrid(jnp.arange(rows), jnp.arange(GRID_W), indexing="ij")
    half = HEAD_DIM // 2
    inv_freq = ROPE_THETA ** (-jnp.arange(0, half, 2, dtype=F32) / half)
    ang_r = row.reshape(-1, 1).astype(F32) * inv_freq
    ang_c = col.reshape(-1, 1).astype(F32) * inv_freq
    ang = jnp.concatenate([ang_r, ang_r, ang_c, ang_c], axis=-1)
    cos = jnp.concatenate([jnp.ones((n_ctx, HEAD_DIM), F32), jnp.cos(ang)], axis=0)
    sin = jnp.concatenate([jnp.zeros((n_ctx, HEAD_DIM), F32), jnp.sin(ang)], axis=0)
    first = (jnp.arange(HEAD_DIM) % 64) < 32
    return cos, jnp.where(first[None, :], -sin, sin)


def kernel(x, c, ctx, c_ctx, w_mod, b_mod, norm1_g, norm2_g, w_in, w_out, dn_conv_w, dn_a_log, dn_dt_bias, dn_norm_g, hg_lb_logits, hg_norm_g, att_q_norm_g, att_k_norm_g, w_router, router_bias, w_exp_gate, w_exp_up, w_exp_down, w_sh_gate, w_sh_up, w_sh_down, final_norm_g):
    bsz, n_lat, d = x.shape
    n_ctx = ctx.shape[1]
    depth = w_in.shape[0]
    t = n_ctx + n_lat
    assert bsz + 1 <= SUBLANES and n_ctx % CHUNK == 0 and n_lat % CHUNK == 0 and n_lat % GRID_W == 0

    cos, sin_s = _rope_tables(n_ctx, n_lat)
    lb_p = jax.nn.softmax(hg_lb_logits.astype(F32), axis=0)
    lower_bounds = jnp.cumsum(lb_p, axis=0) - lb_p[0]

    c_all = jnp.concatenate([c, c_ctx[None, :], jnp.zeros((SUBLANES - bsz - 1, d), c.dtype)], axis=0)
    mod = _mod_vectors(c_all, w_mod, b_mod)
    mod_rows = mod.reshape(depth * SUBLANES * N_MOD, 1, d)

    xs = jnp.concatenate([ctx, x], axis=1)
    for l in range(depth):
        h1 = _norm_mod(xs, norm1_g, mod_rows, l, 0, n_ctx, BF16)
        p = _matmul(h1.reshape(bsz * t, d), _permute_w_in(w_in[l]), F32).reshape(bsz, t, P_WIDTH)
        qr, kr, vr = _att_prep(p, cos, sin_s, att_q_norm_g[l], att_k_norm_g[l])
        og = _attention(qr, kr, vr, n_ctx)
        hf, hb = _hgrn2(p, lower_bounds[l], n_ctx)
        ob = _mix_finish(hf, hb, p, C_HG, hg_norm_g[l])
        dq, dk, dv, gcol, gtot, gcr = _dn_prep(p, dn_conv_w[l], dn_a_log[l], dn_dt_bias[l], n_ctx)
        u, w, qd, kd, qk = _dn_intra(dq, dk, dv, gcol, gtot, gcr)
        df, db = _dn_scan(u, w, qd, kd, qk, gtot, n_ctx)
        oa = _mix_finish(df, db, p, C_DZ, dn_norm_g[l])
        xs = _out_proj(oa, ob, og, w_out[l], xs, mod_rows, l, n_ctx)
        h2 = _norm_mod(xs, norm2_g, mod_rows, l, 3, n_ctx, F32).reshape(bsz * t, d)
        xs = _moe(h2, xs.reshape(bsz * t, d), mod_rows, l, w_router[l], router_bias[l],
                  w_exp_gate[l].astype(BF16), w_exp_up[l].astype(BF16), w_exp_down[l].astype(BF16),
                  w_sh_gate[l].astype(BF16), w_sh_up[l].astype(BF16), w_sh_down[l].astype(BF16),
                  bsz, n_ctx).reshape(bsz, t, d)
    return _final_norm(xs, final_norm_g, n_ctx)
```

```python
import functools

import numpy as np
import jax
import jax.numpy as jnp
from jax import lax
from jax.experimental import pallas as pl
from jax.experimental.pallas import tpu as pltpu

F32 = jnp.float32
BF16 = jnp.bfloat16

EPS = 1e-6
GATE_FLOOR = 1e-30
HEAD_DIM = 128
DN_HEADS = 8
DN_WIDTH = DN_HEADS * HEAD_DIM
DN_CONV = 5
HG_HEADS = 8
HG_WIDTH = HG_HEADS * HEAD_DIM
ATT_Q_HEADS = 16
ATT_KV_HEADS = 4
ATT_GROUP = ATT_Q_HEADS // ATT_KV_HEADS
ATT_Q_WIDTH = ATT_Q_HEADS * HEAD_DIM
ATT_KV_WIDTH = ATT_KV_HEADS * HEAD_DIM
GRID_W = 64
ROPE_THETA = 10000.0
CHUNK = 64
N_EXPERTS = 64
TOP_K = 8
N_GROUPS = 8
TOPK_GROUPS = 4
GROUP_SIZE = N_EXPERTS // N_GROUPS
ROUTED_SCALE = 2.5
N_MOD = 6
LOG2E = 1.4426950408889634

C_AQ = 0
C_AK = C_AQ + ATT_Q_WIDTH
C_AV = C_AK + ATT_KV_WIDTH
C_DQ = C_AV + ATT_KV_WIDTH
C_DZ = C_DQ + 3 * DN_WIDTH
C_HQ = C_DZ + DN_WIDTH
C_HF = C_HQ + HG_WIDTH
C_HI = C_HF + 2 * HG_WIDTH
C_HG = C_HI + HG_WIDTH
C_AB = C_HG + HG_WIDTH
LANES = 128
SUBLANES = 8
P_WIDTH = C_AB + 4 * LANES

MOE_ROWS = 256
COMBINE_COLS = 512
VMEM_LIMIT = 56 * 1024 * 1024


def _cparams(sem, vmem=VMEM_LIMIT):
    return pltpu.CompilerParams(dimension_semantics=sem, vmem_limit_bytes=vmem)


def _pick(n, cap, mult):
    if n <= cap:
        return n
    for t in range(cap - cap % mult, 0, -mult):
        if n % t == 0:
            return t
    raise ValueError(f"no tile for {n} (cap {cap}, multiple of {mult})")


def _silu(x):
    return x * (1.0 / (1.0 + jnp.exp(-x)))


def _sigmoid(x):
    return 1.0 / (1.0 + jnp.exp(-x))


def _dot(a, b):
    return jnp.dot(a, b, preferred_element_type=F32)


def _dot_nt(a, b):
    return lax.dot_general(a, b, (((1,), (1,)), ((), ())), preferred_element_type=F32)


def _dot_tn(a, b):
    return lax.dot_general(a, b, (((0,), (0,)), ((), ())), preferred_element_type=F32)


def _split2(a):
    hi = a.astype(BF16)
    lo = (a - hi.astype(F32)).astype(BF16)
    return hi, lo


_LEVELS = (32, 16, 8, 4, 2, 1)


@functools.lru_cache(maxsize=None)
def _dir_consts():
    c = CHUNK
    t = np.arange(c)
    incl = [(t[None, :] <= t[:, None]), (t[None, :] >= t[:, None])]
    cms, masks = [], []
    for d in range(2):
        md = incl[d].astype(np.float32)
        blocks, mks = [md], []
        for b in _LEVELS:
            blk = t // (2 * b)
            ref = blk * 2 * b + (b - 1 if d == 0 else b)
            blocks.append(md[ref, :])
            same = blk[:, None] == blk[None, :]
            late = (t % (2 * b)) >= b
            if d == 0:
                mks.append(same & late[:, None] & ~late[None, :])
            else:
                mks.append(same & ~late[:, None] & late[None, :])
        mks.append(np.eye(c, dtype=bool))
        blocks.append(np.ones((c, c), np.float32))
        cms.append(np.concatenate(blocks, 0))
        masks.append(np.stack(mks).astype(np.float32))
    cm = np.stack(cms)
    lv = np.stack(masks)
    incl_f = np.stack([m.astype(np.float32) for m in incl])
    bd16 = ((t[:, None] // 16) == (t[None, :] // 16)).astype(np.float32)
    return cm, lv, incl_f, bd16


def _mod_kernel(c_ref, w_ref, b_ref, o_ref):
    a = _silu(c_ref[...]).astype(BF16)
    o_ref[0] = _dot(a, w_ref[0].astype(BF16)) + b_ref[0]


def _mod_vectors(c_all, w_mod, b_mod):
    depth, d, n = w_mod.shape
    tn = _pick(n, 512, LANES)
    return pl.pallas_call(
        _mod_kernel,
        grid=(depth, n // tn),
        in_specs=[
            pl.BlockSpec((SUBLANES, d), lambda l, j: (0, 0)),
            pl.BlockSpec((1, d, tn), lambda l, j: (l, 0, j)),
            pl.BlockSpec((1, 1, tn), lambda l, j: (l, 0, j)),
        ],
        out_specs=pl.BlockSpec((1, SUBLANES, tn), lambda l, j: (l, 0, j)),
        out_shape=jax.ShapeDtypeStruct((depth, SUBLANES, n), F32),
        compiler_params=_cparams(("parallel", "parallel")),
        name="mod_vectors",
    )(c_all, w_mod, b_mod.reshape(depth, 1, n))


def _norm_mod_kernel(x_ref, g_ref, sh_ref, sc_ref, o_ref):
    x = x_ref[0]
    var = jnp.mean(x * x, axis=-1, keepdims=True)
    y = x * lax.rsqrt(var + EPS) * g_ref[0]
    o_ref[0] = (y * (1.0 + sc_ref[0]) + sh_ref[0]).astype(o_ref.dtype)


def _norm_mod(x, gain, mod_rows, layer, k_shift, n_ctx, out_dtype):
    b, t, d = x.shape
    tt = _pick(n_ctx, 256, SUBLANES)
    nct = n_ctx // tt

    def mrow(k):
        return lambda bi, j: ((layer * SUBLANES + jnp.where(j < nct, b, bi)) * N_MOD + k, 0, 0)

    return pl.pallas_call(
        _norm_mod_kernel,
        grid=(b, t // tt),
        in_specs=[
            pl.BlockSpec((1, tt, d), lambda bi, j: (bi, j, 0)),
            pl.BlockSpec((1, 1, d), lambda bi, j: (layer, 0, 0)),
            pl.BlockSpec((1, 1, d), mrow(k_shift)),
            pl.BlockSpec((1, 1, d), mrow(k_shift + 1)),
        ],
        out_specs=pl.BlockSpec((1, tt, d), lambda bi, j: (bi, j, 0)),
        out_shape=jax.ShapeDtypeStruct((b, t, d), out_dtype),
        compiler_params=_cparams(("parallel", "parallel")),
        name="norm_mod",
    )(x, gain.reshape(gain.shape[0], 1, d), mod_rows, mod_rows)


def _final_norm_kernel(x_ref, g_ref, o_ref):
    x = x_ref[0]
    var = jnp.mean(x * x, axis=-1, keepdims=True)
    o_ref[0] = x * lax.rsqrt(var + EPS) * g_ref[...]


def _final_norm(x, gain, n_ctx):
    b, t, d = x.shape
    n_lat = t - n_ctx
    tt = _pick(n_ctx, 256, SUBLANES)
    off = n_ctx // tt
    return pl.pallas_call(
        _final_norm_kernel,
        grid=(b, n_lat // tt),
        in_specs=[
            pl.BlockSpec((1, tt, d), lambda bi, j: (bi, j + off, 0)),
            pl.BlockSpec((1, d), lambda bi, j: (0, 0)),
        ],
        out_specs=pl.BlockSpec((1, tt, d), lambda bi, j: (bi, j, 0)),
        out_shape=jax.ShapeDtypeStruct((b, n_lat, d), F32),
        compiler_params=_cparams(("parallel", "parallel")),
        name="final_norm",
    )(x, gain.reshape(1, d))


def _mm_kernel(a_ref, w_ref, o_ref):
    o_ref[...] = _dot(a_ref[...], w_ref[...]).astype(o_ref.dtype)


def _matmul(a, w, out_dtype):
    m, k = a.shape
    n = w.shape[1]
    tm = _pick(m, 1536, 256) if m > 1536 else m
    tn = _pick(n, 512, LANES)
    return pl.pallas_call(
        _mm_kernel,
        grid=(m // tm, n // tn),
        in_specs=[
            pl.BlockSpec((tm, k), lambda i, j: (i, 0)),
            pl.BlockSpec((k, tn), lambda i, j: (0, j)),
        ],
        out_specs=pl.BlockSpec((tm, tn), lambda i, j: (i, j)),
        out_shape=jax.ShapeDtypeStruct((m, n), out_dtype),
        compiler_params=_cparams(("parallel", "parallel")),
        name="in_proj",
    )(a, w)


def _permute_w_in(w):
    d = w.shape[0]
    dn_cols = 4 * DN_WIDTH + 4 * DN_HEADS
    hg_cols = 5 * HG_WIDTH
    att0 = dn_cols + hg_cols
    parts = [
        w[:, att0:],
        w[:, :4 * DN_WIDTH],
        w[:, dn_cols:att0],
        w[:, 4 * DN_WIDTH:dn_cols],
        jnp.zeros((d, P_WIDTH - C_AB - 4 * DN_HEADS), w.dtype),
    ]
    return jnp.concatenate(parts, axis=1).astype(BF16)


def _att_prep_kernel(q_ref, k_ref, v_ref, cos_ref, sin_ref, qg_ref, kg_ref, qo_ref, ko_ref, vo_ref):
    cos = cos_ref[...]
    sin = sin_ref[...]
    lane = lax.broadcasted_iota(jnp.int32, cos.shape, 1)
    first = (lane % 64) < 32

    def norm_rope(x, g, scale):
        y = x * lax.rsqrt(jnp.mean(x * x, axis=-1, keepdims=True) + EPS) * g
        r = jnp.where(first, pltpu.roll(y, HEAD_DIM - 32, 1), pltpu.roll(y, 32, 1))
        return (y * cos + r * sin) * scale

    q = q_ref[0]
    qg = qg_ref[...]
    for h in range(ATT_Q_HEADS):
        sl = slice(h * HEAD_DIM, (h + 1) * HEAD_DIM)
        qo_ref[0, :, sl] = norm_rope(q[:, sl], qg, HEAD_DIM ** -0.5 * LOG2E).astype(qo_ref.dtype)
    k = k_ref[0]
    kg = kg_ref[...]
    for h in range(ATT_KV_HEADS):
        sl = slice(h * HEAD_DIM, (h + 1) * HEAD_DIM)
        ko_ref[0, :, sl] = norm_rope(k[:, sl], kg, 1.0).astype(ko_ref.dtype)
    vo_ref[0] = v_ref[0].astype(vo_ref.dtype)


def _att_prep(p, cos, sin_s, qg, kg):
    b, t, _ = p.shape
    tt = _pick(t, 256, SUBLANES)
    kvb = C_AK // ATT_KV_WIDTH
    return pl.pallas_call(
        _att_prep_kernel,
        grid=(b, t // tt),
        in_specs=[
            pl.BlockSpec((1, tt, ATT_Q_WIDTH), lambda bi, j: (bi, j, 0)),
            pl.BlockSpec((1, tt, ATT_KV_WIDTH), lambda bi, j: (bi, j, kvb)),
            pl.BlockSpec((1, tt, ATT_KV_WIDTH), lambda bi, j: (bi, j, kvb + 1)),
            pl.BlockSpec((tt, HEAD_DIM), lambda bi, j: (j, 0)),
            pl.BlockSpec((tt, HEAD_DIM), lambda bi, j: (j, 0)),
            pl.BlockSpec((1, HEAD_DIM), lambda bi, j: (0, 0)),
            pl.BlockSpec((1, HEAD_DIM), lambda bi, j: (0, 0)),
        ],
        out_specs=[
            pl.BlockSpec((1, tt, ATT_Q_WIDTH), lambda bi, j: (bi, j, 0)),
            pl.BlockSpec((1, tt, ATT_KV_WIDTH), lambda bi, j: (bi, j, 0)),
            pl.BlockSpec((1, tt, ATT_KV_WIDTH), lambda bi, j: (bi, j, 0)),
        ],
        out_shape=[
            jax.ShapeDtypeStruct((b, t, ATT_Q_WIDTH), BF16),
            jax.ShapeDtypeStruct((b, t, ATT_KV_WIDTH), BF16),
            jax.ShapeDtypeStruct((b, t, ATT_KV_WIDTH), BF16),
        ],
        compiler_params=_cparams(("parallel", "parallel")),
        name="att_prep",
    )(p, p, p, cos, sin_s, qg.reshape(1, HEAD_DIM), kg.reshape(1, HEAD_DIM))


ATT_ROW_BLOCK = 512
ATT_KEY_TILE = 256
ATT_KEY_CHUNK = 1024


def _attn_kernel(q_ref, k_ref, v_ref, o_ref, qs_ref, sa_ref, sb_ref, pa_ref, pb_ref, ala_ref, alb_ref,
                 m_ref, l_ref, acc_ref, *, tq, tk, n_ctx, lat_chunks, ctx_qtiles):
    i = pl.program_id(2)
    rows = ATT_GROUP * tq
    for g in range(ATT_GROUP):
        qs_ref[g * tq:(g + 1) * tq, :] = q_ref[0, :, g * HEAD_DIM:(g + 1) * HEAD_DIM]

    n_blocks = rows // ATT_ROW_BLOCK

    def max_pass(s_ref, al_ref, width, first, blocks):
        for r in blocks:
            rs = slice(r * ATT_ROW_BLOCK, (r + 1) * ATT_ROW_BLOCK)
            mx = s_ref[rs, 0:LANES]
            for c0 in range(LANES, width, LANES):
                mx = jnp.maximum(mx, s_ref[rs, c0:c0 + LANES])
            m_new = jnp.max(mx, axis=1, keepdims=True)
            if first:
                al_ref[rs, :] = jnp.ones_like(m_new)
            else:
                m_old = m_ref[rs, :]
                m_new = jnp.maximum(m_old, m_new)
                al_ref[rs, :] = jnp.exp2(m_old - m_new)
            m_ref[rs, :] = m_new

    def exp_pass(s_ref, p_ref, al_ref, width, first, blocks):
        for r in blocks:
            rs = slice(r * ATT_ROW_BLOCK, (r + 1) * ATT_ROW_BLOCK)
            m_new = m_ref[rs, :]
            psum = jnp.zeros((ATT_ROW_BLOCK, LANES), F32)
            for c0 in range(0, width, LANES):
                pr = jnp.exp2(s_ref[rs, c0:c0 + LANES] - m_new)
                psum = psum + pr
                p_ref[rs, c0:c0 + LANES] = pr.astype(p_ref.dtype)
            lsum = jnp.sum(psum, axis=1, keepdims=True)
            if first:
                l_ref[rs, :] = lsum
            else:
                l_ref[rs, :] = al_ref[rs, :] * l_ref[rs, :] + lsum

    def softmax_update(s_ref, p_ref, al_ref, width, first):
        max_pass(s_ref, al_ref, width, first, range(n_blocks))
        exp_pass(s_ref, p_ref, al_ref, width, first, range(n_blocks))

    def write_out():
        out = acc_ref[...] * (1.0 / l_ref[...])
        for g in range(ATT_GROUP):
            o_ref[0, :, g * HEAD_DIM:(g + 1) * HEAD_DIM] = out[g * tq:(g + 1) * tq].astype(o_ref.dtype)

    sb_ref[:, :n_ctx] = _dot_nt(qs_ref[...], k_ref[0, 0:n_ctx, :])

    @pl.when(i < ctx_qtiles)
    def _():
        softmax_update(sb_ref, pb_ref, alb_ref, n_ctx, True)
        acc_ref[...] = _dot(pb_ref[:, :n_ctx], v_ref[0, 0:n_ctx, :])
        write_out()

    @pl.when(i >= ctx_qtiles)
    def _():
        def rows_at(ref, start):
            if isinstance(start, int):
                return ref[0, start:start + tk, :]
            return ref[0, pl.ds(pl.multiple_of(start, LANES), tk), :]

        def k_chunk(c):
            return rows_at(k_ref, n_ctx + c * tk)

        def v_before(c):
            start = n_ctx + (c - 1) * tk
            return rows_at(v_ref, max(start, 0) if isinstance(start, int) else jnp.maximum(start, 0))

        def pv(p_ref, al_ref, vc):
            acc_ref[...] = al_ref[...] * acc_ref[...] + _dot(p_ref[...], vc)

        def stage(c, s_cur, p_cur, al_cur, s_nxt, p_prev, al_prev):
            pv(p_prev, al_prev, v_before(c))
            max_pass(s_cur, al_cur, tk, False, range(n_blocks))
            kt = min(ATT_KEY_TILE, tk)
            parts = tk // kt
            k_next = k_chunk(c + 1) if s_nxt is not None else None
            for j in range(parts):
                if s_nxt is not None:
                    ks = slice(j * kt, (j + 1) * kt)
                    s_nxt[:, ks] = _dot_nt(qs_ref[...], k_next[ks])
                exp_pass(s_cur, p_cur, al_cur, tk, False,
                         range(j * n_blocks // parts, (j + 1) * n_blocks // parts))

        a_bufs = (sa_ref, pa_ref, ala_ref)
        b_bufs = (sb_ref, pb_ref, alb_ref)
        if n_ctx < tk:
            pb_ref[:, n_ctx:] = jnp.zeros((rows, tk - n_ctx), pb_ref.dtype)
        acc_ref[...] = jnp.zeros_like(acc_ref)
        sa_ref[...] = _dot_nt(qs_ref[...], k_chunk(0))
        softmax_update(sb_ref, pb_ref, alb_ref, n_ctx, True)

        def pair(cp, carry):
            c0 = 2 * cp
            stage(c0, *a_bufs, sb_ref, pb_ref, alb_ref)
            stage(c0 + 1, *b_bufs, sa_ref, pa_ref, ala_ref)
            return carry

        lax.fori_loop(0, lat_chunks // 2 - 1, pair, 0)
        stage(lat_chunks - 2, *a_bufs, sb_ref, pb_ref, alb_ref)
        stage(lat_chunks - 1, *b_bufs, None, pa_ref, ala_ref)
        pv(pb_ref, alb_ref, v_before(lat_chunks))
        write_out()


def _attention(qr, kr, vr, n_ctx):
    b, t, _ = qr.shape
    n_lat = t - n_ctx
    tq = _pick(n_ctx, 256, SUBLANES)
    tk = _pick(n_lat // 2, ATT_KEY_CHUNK, LANES)
    assert n_lat % (2 * tk) == 0 and n_ctx % LANES == 0 and n_ctx <= tk
    rows = ATT_GROUP * tq
    gw = ATT_GROUP * HEAD_DIM
    kern = functools.partial(_attn_kernel, tq=tq, tk=tk, n_ctx=n_ctx, lat_chunks=n_lat // tk,
                             ctx_qtiles=n_ctx // tq)
    return pl.pallas_call(
        kern,
        grid=(b, ATT_KV_HEADS, t // tq),
        in_specs=[
            pl.BlockSpec((1, tq, gw), lambda bi, h, i: (bi, i, h)),
            pl.BlockSpec((1, t, HEAD_DIM), lambda bi, h, i: (bi, 0, h)),
            pl.BlockSpec((1, t, HEAD_DIM), lambda bi, h, i: (bi, 0, h)),
        ],
        out_specs=pl.BlockSpec((1, tq, gw), lambda bi, h, i: (bi, i, h)),
        out_shape=jax.ShapeDtypeStruct((b, t, ATT_Q_WIDTH), BF16),
        scratch_shapes=[
            pltpu.VMEM((rows, HEAD_DIM), BF16),
            pltpu.VMEM((rows, tk), F32), pltpu.VMEM((rows, tk), F32),
            pltpu.VMEM((rows, tk), BF16), pltpu.VMEM((rows, tk), BF16),
            pltpu.VMEM((rows, 1), F32), pltpu.VMEM((rows, 1), F32),
            pltpu.VMEM((rows, 1), F32), pltpu.VMEM((rows, 1), F32),
            pltpu.VMEM((rows, HEAD_DIM), F32),
        ],
        compiler_params=_cparams(("parallel", "parallel", "parallel")),
        name="attention",
    )(qr, kr, vr)


def _chunk_maps(n_ctx, t, size=CHUNK):
    ncc = n_ctx // size
    nc = t // size

    def fwd(i):
        return i

    def bwd(i):
        return jnp.where(i < ncc, ncc - 1 - i, nc - 1 + ncc - i)

    return fwd, bwd, nc


def _bdot_tn(a, b):
    return lax.dot_general(a, b, (((1,), (1,)), ((0,), (0,))), preferred_element_type=F32)


def _hg_kernel(qf_ref, ff_ref, if_ref, qb_ref, fb_ref, ib_ref, lbf_ref, lbb_ref, cm_ref, lv_ref,
               of_ref, ob_ref, s_ref, *, nch):
    @pl.when(pl.program_id(2) == 0)
    def _():
        s_ref[...] = jnp.zeros_like(s_ref)

    c = CHUNK
    split = lambda a: a.reshape(nch, c, HEAD_DIM)

    def gates(q_raw, f_raw, lb):
        one_m_lb = 1.0 - lb
        f_gate = lb + one_m_lb * _sigmoid(f_raw)
        log_f = jnp.log(jnp.maximum(f_gate, GATE_FLOOR))
        return split(_silu(q_raw)), split(one_m_lb * _sigmoid(-f_raw)), split(log_f)

    parts = [gates(qf_ref[0], ff_ref[0], lbf_ref[0]), gates(qb_ref[0], fb_ref[0], lbb_ref[0])]
    q, key, log_f = (jnp.concatenate([parts[0][n], parts[1][n]], axis=0) for n in range(3))
    vb = jnp.concatenate([split(if_ref[0]), split(ib_ref[0])], axis=0).astype(BF16)
    hi = log_f.astype(BF16)
    r1 = log_f - hi.astype(F32)
    mid = r1.astype(BF16)
    lo = (r1 - mid.astype(F32)).astype(BF16)
    cm = cm_ref[...]
    x = _bdot(cm, hi) + _bdot(cm, mid) + _bdot(cm, lo)
    cum = x[:, 0:c]
    tot = x[:, 7 * c:7 * c + 1]
    lv = lv_ref[...]
    att = jnp.where(lv[:, 6] > 0, _bdot_nt(q.astype(BF16), key.astype(BF16)), 0.0)
    for n in range(len(_LEVELS)):
        ref = x[:, (n + 1) * c:(n + 2) * c]
        qs = (q * jnp.exp(jnp.minimum(cum - ref, 0.0))).astype(BF16)
        ks = (key * jnp.exp(jnp.minimum(ref - cum, 0.0))).astype(BF16)
        att = att + jnp.where(lv[:, n] > 0, _bdot_nt(qs, ks), 0.0)
    o_intra = _bdot(att.astype(BF16), vb)
    q_dec = (q * jnp.exp(cum)).astype(BF16)
    k_dec = (key * jnp.exp(tot - cum)).astype(BF16)
    kv = _bdot_tn(vb, k_dec)
    decay = jnp.exp(tot)
    for d, o_ref in ((0, of_ref), (1, ob_ref)):
        st = s_ref[d]
        for ci in (range(nch) if d == 0 else range(nch - 1, -1, -1)):
            n = d * nch + ci
            o_ref[0, ci * c:(ci + 1) * c, :] = o_intra[n] + _dot_nt(q_dec[n], st.astype(BF16))
            st = st * decay[n] + kv[n]
        s_ref[d] = st


def _hgrn2(p, lower_bound, n_ctx):
    b, t, _ = p.shape
    nch = min(4, n_ctx // CHUNK)
    tb = nch * CHUNK
    fwd, bwd, nblk = _chunk_maps(n_ctx, t, tb)
    cm, lv, _, _ = _dir_consts()
    cm = jnp.asarray(np.repeat(cm, nch, axis=0), BF16)
    lv = jnp.asarray(np.repeat(lv, nch, axis=0), F32)
    lb = lower_bound.astype(F32).reshape(2 * HG_HEADS, 1, HEAD_DIM)
    hq, hf, hi_ = C_HQ // HEAD_DIM, C_HF // HEAD_DIM, C_HI // HEAD_DIM
    blk = (1, tb, HEAD_DIM)

    def col(base, order):
        return pl.BlockSpec(blk, lambda bi, h, i: (bi, order(i), base + h))

    full = lambda shape: pl.BlockSpec(shape, lambda bi, h, i: (0,) * len(shape))
    return pl.pallas_call(
        functools.partial(_hg_kernel, nch=nch),
        grid=(b, HG_HEADS, nblk),
        in_specs=[
            col(hq, fwd), col(hf, fwd), col(hi_, fwd),
            col(hq, bwd), col(hf + HG_HEADS, bwd), col(hi_, bwd),
            pl.BlockSpec((1, 1, HEAD_DIM), lambda bi, h, i: (h, 0, 0)),
            pl.BlockSpec((1, 1, HEAD_DIM), lambda bi, h, i: (HG_HEADS + h, 0, 0)),
            full(cm.shape), full(lv.shape),
        ],
        out_specs=[
            pl.BlockSpec(blk, lambda bi, h, i: (bi, fwd(i), h)),
            pl.BlockSpec(blk, lambda bi, h, i: (bi, bwd(i), h)),
        ],
        out_shape=[jax.ShapeDtypeStruct((b, t, HG_WIDTH), F32)] * 2,
        scratch_shapes=[pltpu.VMEM((2, HEAD_DIM, HEAD_DIM), F32)],
        compiler_params=_cparams(("parallel", "parallel", "arbitrary")),
        name="hgrn2",
    )(p, p, p, p, p, p, lb, lb, cm, lv)


def _dn_prep_kernel(x_ref, prev_ref, next_ref, ab_ref, w_ref, alog_ref, dt_ref, cs_ref,
                    q_ref, k_ref, v_ref, gcol_ref, gtot_ref, gct_ref, *, tt, ctx_tiles, n_tiles):
    j = pl.program_id(1)
    seg_start = jnp.logical_or(j == 0, j == ctx_tiles)
    seg_end = jnp.logical_or(j == ctx_tiles - 1, j == n_tiles - 1)
    x = x_ref[0]
    ph = jnp.where(seg_start, 0.0, prev_ref[0])
    nh = jnp.where(seg_end, 0.0, next_ref[0])
    w = w_ref[0]
    r8 = lax.broadcasted_iota(jnp.int32, (SUBLANES, x.shape[1]), 0)
    half = DN_CONV // 2

    def later(s):
        r = pltpu.roll(x, tt - s, 0)
        tail = jnp.where(r8 >= SUBLANES - s, pltpu.roll(nh, SUBLANES - s, 0), r[tt - SUBLANES:])
        return jnp.concatenate([r[:tt - SUBLANES], tail], axis=0)

    def earlier(s):
        r = pltpu.roll(x, s, 0)
        head = jnp.where(r8 < s, pltpu.roll(ph, s, 0), r[:SUBLANES])
        return jnp.concatenate([head, r[SUBLANES:]], axis=0)

    acc = x * w[half:half + 1]
    for s in range(1, half + 1):
        acc = acc + later(s) * w[half + s:half + s + 1] + earlier(s) * w[half - s:half - s + 1]
    y = _silu(acc)
    for h in range(DN_HEADS):
        qh = y[:, h * HEAD_DIM:(h + 1) * HEAD_DIM]
        kh = y[:, DN_WIDTH + h * HEAD_DIM:DN_WIDTH + (h + 1) * HEAD_DIM]
        q_ref[0, h] = qh * lax.rsqrt(jnp.sum(qh * qh, axis=-1, keepdims=True) + EPS) * HEAD_DIM ** -0.5
        k_ref[0, h] = kh * lax.rsqrt(jnp.sum(kh * kh, axis=-1, keepdims=True) + EPS)
        v_ref[0, h] = y[:, 2 * DN_WIDTH + h * HEAD_DIM:2 * DN_WIDTH + (h + 1) * HEAD_DIM]
    a = ab_ref[0]
    z = a + dt_ref[...]
    softplus = jnp.maximum(z, 0.0) + jnp.log1p(jnp.exp(-jnp.abs(z)))
    lane = lax.broadcasted_iota(jnp.int32, a.shape, 1)
    gb = jnp.where(lane < 2 * DN_HEADS, -jnp.exp(alog_ref[...]) * softplus, _sigmoid(a))
    hi = gb.astype(BF16)
    r1 = gb - hi.astype(F32)
    mid = r1.astype(BF16)
    lo = (r1 - mid.astype(F32)).astype(BF16)
    cmat = cs_ref[...]
    xs = _dot(cmat, hi) + _dot(cmat, mid) + _dot(cmat, lo)
    gcol = jnp.where(lane < DN_HEADS, xs[:tt], jnp.where(lane < 2 * DN_HEADS, xs[tt:2 * tt], gb))
    gcol_ref[0] = gcol
    gtot_ref[0] = xs[2 * tt:]
    gct_ref[0] = gcol.T


def _dn_prep(p, conv_w, a_log, dt_bias, n_ctx):
    b, t, _ = p.shape
    tt = _pick(n_ctx, 128, LANES)
    n_tiles = t // tt
    r = tt // SUBLANES
    last8 = t // SUBLANES - 1
    cw = 3 * DN_WIDTH
    cb = C_DQ // cw
    pad = lambda v: jnp.pad(v.astype(F32).reshape(1, -1), ((0, 0), (0, LANES - 2 * DN_HEADS)))
    kern = functools.partial(_dn_prep_kernel, tt=tt, ctx_tiles=n_ctx // tt, n_tiles=n_tiles)
    hshape = jax.ShapeDtypeStruct((b, DN_HEADS, t, HEAD_DIM), F32)
    hspec = pl.BlockSpec((1, DN_HEADS, tt, HEAD_DIM), lambda bi, j: (bi, 0, j, 0))
    _, _, incl, _ = _dir_consts()
    eye = np.eye(tt // CHUNK, dtype=np.float32)
    cs = np.concatenate([np.kron(eye, incl[0]), np.kron(eye, incl[1]),
                         np.kron(eye, np.ones((CHUNK, CHUNK), np.float32))], axis=0)
    tile = pl.BlockSpec((1, tt, LANES), lambda bi, j: (bi, j, 0))
    q, k, v, gcol, gtot, gct = pl.pallas_call(
        kern,
        grid=(b, n_tiles),
        in_specs=[
            pl.BlockSpec((1, tt, cw), lambda bi, j: (bi, j, cb)),
            pl.BlockSpec((1, SUBLANES, cw), lambda bi, j: (bi, jnp.maximum(j * r - 1, 0), cb)),
            pl.BlockSpec((1, SUBLANES, cw), lambda bi, j: (bi, jnp.minimum((j + 1) * r, last8), cb)),
            pl.BlockSpec((1, tt, LANES), lambda bi, j: (bi, j, C_AB // LANES)),
            pl.BlockSpec((1, DN_CONV, cw), lambda bi, j: (0, 0, 0)),
            pl.BlockSpec((1, LANES), lambda bi, j: (0, 0)),
            pl.BlockSpec((1, LANES), lambda bi, j: (0, 0)),
            pl.BlockSpec((3 * tt, tt), lambda bi, j: (0, 0)),
        ],
        out_specs=[hspec, hspec, hspec, tile, tile,
                   pl.BlockSpec((1, LANES, tt), lambda bi, j: (bi, 0, j))],
        out_shape=[hshape, hshape, hshape,
                   jax.ShapeDtypeStruct((b, t, LANES), F32),
                   jax.ShapeDtypeStruct((b, t, LANES), F32),
                   jax.ShapeDtypeStruct((b, LANES, t), F32)],
        compiler_params=_cparams(("parallel", "parallel")),
        name="dn_prep",
    )(p, p, p, p, conv_w.astype(F32)[None], pad(a_log), pad(dt_bias), jnp.asarray(cs, BF16))
    gcr = gct[:, :2 * DN_HEADS].reshape(b, 2 * DN_HEADS, t // CHUNK, CHUNK).transpose(0, 2, 1, 3)
    return q, k, v, gcol, gtot, gcr


def _bdot(a, b):
    return lax.dot_general(a, b, (((2,), (1,)), ((0,), (0,))), preferred_element_type=F32)


def _bdot_nt(a, b):
    return lax.dot_general(a, b, (((2,), (2,)), ((0,), (0,))), preferred_element_type=F32)


def _bdot3(a, b):
    ah, al = _split2(a)
    bh, bl = _split2(b)
    return _bdot(ah, bh) + _bdot(ah, bl) + _bdot(al, bh)


def _tri_inverse(a, eye, bd16, off32, off64):
    n = -(a * bd16)
    s = eye + n
    p = n
    for _ in range(3):
        p = _bdot3(p, p)
        s = s + _bdot3(p, s)
    for off in (off32, off64):
        s = s - _bdot3(_bdot3(s, a * off), s)
    return s


def _dn_intra_kernel(q_ref, k_ref, v_ref, gcol_ref, gtot_ref, gcr_ref, incl_ref, off32_ref, off64_ref, bd_ref,
                     u_ref, w_ref, qd_ref, kd_ref, qk_ref, *, nch):
    h = pl.program_id(1)
    c = CHUNK
    tb = nch * c
    split = lambda a: a.reshape(nch, c, a.shape[-1])
    both = lambda a: jnp.concatenate([a, a], axis=0)
    q3, k3, v3 = split(q_ref[0, 0]), split(k_ref[0, 0]), split(v_ref[0, 0])
    gcol = gcol_ref[0]
    gtot = gtot_ref[0]
    lane = lax.broadcasted_iota(jnp.int32, gcol.shape, 1)
    pick = lambda a, j: split(jnp.sum(jnp.where(lane == j, a, 0.0), axis=1, keepdims=True))
    js = [d * DN_HEADS + h for d in range(2)]
    gc_c = jnp.concatenate([pick(gcol, j) for j in js], axis=0)
    beta = jnp.concatenate([pick(gcol, 2 * DN_HEADS + j) for j in js], axis=0)
    tot = jnp.concatenate([pick(gtot, j) for j in js], axis=0)
    gc_r = jnp.concatenate([gcr_ref[0, :, pl.ds(j, 1), :] for j in js], axis=0)
    q2, k2, v2 = both(q3), both(k3), both(v3)
    incl = incl_ref[...]
    eye = (lax.broadcasted_iota(jnp.int32, (c, c), 0) == lax.broadcasted_iota(jnp.int32, (c, c), 1)).astype(F32)
    decay = jnp.where(incl > 0, jnp.exp(jnp.minimum(gc_c - gc_r, 0.0)), 0.0)
    kb = k2 * beta
    k2b = k2.astype(BF16)
    a = (incl - eye) * _bdot_nt(kb.astype(BF16), k2b) * decay
    t_inv = _tri_inverse(a, eye, bd_ref[...], off32_ref[...], off64_ref[...]).astype(BF16)
    e_gc = jnp.exp(gc_c)
    u = _bdot(t_inv, (v2 * beta).astype(BF16))
    w = _bdot(t_inv, (kb * e_gc).astype(BF16))
    qd = q2 * e_gc
    kd = k2 * jnp.exp(tot - gc_c)
    qk = both(_bdot_nt(q3.astype(BF16), k3.astype(BF16))) * decay
    for d in range(2):
        sl = slice(d * nch, (d + 1) * nch)
        u_ref[d, 0, 0] = u[sl].reshape(tb, HEAD_DIM)
        w_ref[d, 0, 0] = w[sl].reshape(tb, HEAD_DIM).astype(w_ref.dtype)
        qd_ref[d, 0, 0] = qd[sl].reshape(tb, HEAD_DIM).astype(qd_ref.dtype)
        kd_ref[d, 0, 0] = kd[sl].reshape(tb, HEAD_DIM).astype(kd_ref.dtype)
        qk_ref[d, 0, 0] = qk[sl].reshape(tb, c).astype(qk_ref.dtype)


def _dn_intra(q, k, v, gcol, gtot, gcr):
    b, nh, t, _ = q.shape
    tb = _pick(t, 256, CHUNK)
    nch = tb // CHUNK
    _, lv, incl, bd16 = _dir_consts()
    rep = lambda m: jnp.asarray(np.repeat(m, nch, axis=0))
    incl2, off32, off64 = rep(incl), rep(lv[:, 1]), rep(lv[:, 0])
    bd16 = jnp.asarray(bd16)
    hspec = pl.BlockSpec((1, 1, tb, HEAD_DIM), lambda bi, h, i: (bi, h, i, 0))
    tile = pl.BlockSpec((1, tb, LANES), lambda bi, h, i: (bi, i, 0))
    full = lambda a: pl.BlockSpec(a.shape, lambda bi, h, i: (0,) * a.ndim)
    ospec = pl.BlockSpec((2, 1, 1, tb, HEAD_DIM), lambda bi, h, i: (0, bi, h, i, 0))
    oshape = lambda dt: jax.ShapeDtypeStruct((2, b, nh, t, HEAD_DIM), dt)
    return pl.pallas_call(
        functools.partial(_dn_intra_kernel, nch=nch),
        grid=(b, nh, t // tb),
        in_specs=[
            hspec, hspec, hspec, tile, tile,
            pl.BlockSpec((1, nch, 2 * DN_HEADS, CHUNK), lambda bi, h, i: (bi, i, 0, 0)),
            full(incl2), full(off32), full(off64), full(bd16),
        ],
        out_specs=[ospec, ospec, ospec, ospec,
                   pl.BlockSpec((2, 1, 1, tb, CHUNK), lambda bi, h, i: (0, bi, h, i, 0))],
        out_shape=[oshape(F32), oshape(BF16), oshape(BF16), oshape(BF16),
                   jax.ShapeDtypeStruct((2, b, nh, t, CHUNK), BF16)],
        compiler_params=_cparams(("parallel", "parallel", "parallel")),
        name="dn_intra",
    )(q, k, v, gcol, gtot, gcr, incl2, off32, off64, bd16)


def _dn_scan_kernel(*refs):
    ins = refs[:12]
    of_ref, ob_ref, s_ref = refs[12:]

    @pl.when(pl.program_id(1) == 0)
    def _():
        s_ref[...] = jnp.zeros_like(s_ref)

    for d, o_ref in ((0, of_ref), (1, ob_ref)):
        u_ref, w_ref, qd_ref, kd_ref, qk_ref, gtot_ref = ins[6 * d:6 * d + 6]
        last = jnp.exp(gtot_ref[0, 0:1, :])
        for h in range(DN_HEADS):
            s = s_ref[d, h]
            wq = jnp.concatenate([w_ref[0, 0, h], qd_ref[0, 0, h]], axis=0)
            r = _dot(wq, s.astype(BF16))
            v_new = (u_ref[0, 0, h] - r[:CHUNK]).astype(BF16)
            o_ref[0, :, h * HEAD_DIM:(h + 1) * HEAD_DIM] = r[CHUNK:] + _dot(qk_ref[0, 0, h], v_new)
            j = d * DN_HEADS + h
            s_ref[d, h] = s * last[:, j:j + 1] + _dot_tn(kd_ref[0, 0, h], v_new)


def _dn_scan(u, w, qd, kd, qk, gtot, n_ctx):
    _, b, nh, t, _ = u.shape
    fwd, bwd, nc = _chunk_maps(n_ctx, t)

    def specs(d, order):
        hs = lambda width: pl.BlockSpec((1, 1, nh, CHUNK, width), lambda bi, i: (d, bi, 0, order(i), 0))
        return [hs(HEAD_DIM)] * 4 + [hs(CHUNK), pl.BlockSpec((1, CHUNK, LANES), lambda bi, i: (bi, order(i), 0))]

    args = (u, w, qd, kd, qk, gtot)
    return pl.pallas_call(
        _dn_scan_kernel,
        grid=(b, nc),
        in_specs=specs(0, fwd) + specs(1, bwd),
        out_specs=[
            pl.BlockSpec((1, CHUNK, DN_WIDTH), lambda bi, i: (bi, fwd(i), 0)),
            pl.BlockSpec((1, CHUNK, DN_WIDTH), lambda bi, i: (bi, bwd(i), 0)),
        ],
        out_shape=[jax.ShapeDtypeStruct((b, t, DN_WIDTH), F32)] * 2,
        scratch_shapes=[pltpu.VMEM((2, nh, HEAD_DIM, HEAD_DIM), F32)],
        compiler_params=_cparams(("parallel", "arbitrary")),
        name="dn_scan",
    )(*args, *args)


def _mix_finish_kernel(of_ref, ob_ref, z_ref, g_ref, o_ref):
    g = g_ref[...]
    for h in range(of_ref.shape[2] // HEAD_DIM):
        sl = slice(h * HEAD_DIM, (h + 1) * HEAD_DIM)
        o = of_ref[0, :, sl] + ob_ref[0, :, sl]
        y = o * lax.rsqrt(jnp.mean(o * o, axis=-1, keepdims=True) + EPS) * g
        o_ref[0, :, sl] = (y * _silu(z_ref[0, :, sl])).astype(o_ref.dtype)


def _mix_finish(o_f, o_b, p, gate_col, gain):
    b, t, width = o_f.shape
    tt = _pick(t, 256, SUBLANES)
    spec = pl.BlockSpec((1, tt, width), lambda bi, j: (bi, j, 0))
    return pl.pallas_call(
        _mix_finish_kernel,
        grid=(b, t // tt),
        in_specs=[spec, spec,
                  pl.BlockSpec((1, tt, width), lambda bi, j: (bi, j, gate_col // width)),
                  pl.BlockSpec((1, HEAD_DIM), lambda bi, j: (0, 0))],
        out_specs=spec,
        out_shape=jax.ShapeDtypeStruct((b, t, width), BF16),
        compiler_params=_cparams(("parallel", "parallel")),
        name="mix_finish",
    )(o_f, o_b, p, gain.astype(F32).reshape(1, HEAD_DIM))


def _out_proj_kernel(a1_ref, a2_ref, a3_ref, w1_ref, w2_ref, w3_ref, x_ref, g_ref, o_ref):
    y = _dot(a1_ref[0], w1_ref[...]) + _dot(a2_ref[0], w2_ref[...]) + _dot(a3_ref[0], w3_ref[...])
    o_ref[0] = x_ref[0] + g_ref[0] * y


def _out_proj(oa, ob, og, w_out, x, mod_rows, layer, n_ctx):
    b, t, d = x.shape
    tm = _pick(n_ctx, 256, SUBLANES)
    tn = _pick(d, 1024, LANES)
    nct = n_ctx // tm
    w = w_out.astype(BF16)
    return pl.pallas_call(
        _out_proj_kernel,
        grid=(d // tn, b, t // tm),
        in_specs=[
            pl.BlockSpec((1, tm, DN_WIDTH), lambda n, bi, i: (bi, i, 0)),
            pl.BlockSpec((1, tm, HG_WIDTH), lambda n, bi, i: (bi, i, 0)),
            pl.BlockSpec((1, tm, ATT_Q_WIDTH), lambda n, bi, i: (bi, i, 0)),
            pl.BlockSpec((DN_WIDTH, tn), lambda n, bi, i: (0, n)),
            pl.BlockSpec((HG_WIDTH, tn), lambda n, bi, i: (1, n)),
            pl.BlockSpec((ATT_Q_WIDTH, tn), lambda n, bi, i: (1, n)),
            pl.BlockSpec((1, tm, tn), lambda n, bi, i: (bi, i, n)),
            pl.BlockSpec((1, 1, tn), lambda n, bi, i: (
                (layer * SUBLANES + jnp.where(i < nct, b, bi)) * N_MOD + 2, 0, n)),
        ],
        out_specs=pl.BlockSpec((1, tm, tn), lambda n, bi, i: (bi, i, n)),
        out_shape=jax.ShapeDtypeStruct((b, t, d), F32),
        compiler_params=_cparams(("parallel", "parallel", "parallel")),
        name="out_proj",
    )(oa, ob, og, w, w, w, x, mod_rows)


def _router_kernel(h_ref, w_ref, b_ref, idx_ref, wt_ref, rank_ref, cnt_ref, carry_ref):
    i = pl.program_id(0)

    @pl.when(i == 0)
    def _():
        carry_ref[...] = jnp.zeros_like(carry_ref)

    tm = h_ref.shape[0]
    neg = -jnp.inf
    logits = _dot(h_ref[...].astype(BF16), w_ref[...])
    scores = _sigmoid(logits)
    lane = lax.broadcasted_iota(jnp.int32, scores.shape, 1)
    valid = lane < N_EXPERTS
    grp = lane // GROUP_SIZE
    big = jnp.int32(1 << 20)
    sel = jnp.where(valid, scores + b_ref[...], neg)
    rmax = lambda a: jnp.max(a, axis=1, keepdims=True)
    rmin = lambda a: jnp.min(a, axis=1, keepdims=True)

    grp_score = jnp.full(scores.shape, neg, F32)
    for g in range(N_GROUPS):
        in_g = grp == g
        cur = jnp.where(in_g, sel, neg)
        m1 = rmax(cur)
        i1 = rmin(jnp.where(cur == m1, lane, big))
        m2 = rmax(jnp.where(lane == i1, neg, cur))
        grp_score = jnp.where(in_g, m1 + m2, grp_score)
    keep = jnp.zeros(scores.shape, jnp.bool_)
    cur = grp_score
    for _ in range(TOPK_GROUPS):
        m = rmax(cur)
        gsel = rmin(jnp.where(cur == m, grp, big))
        hit = grp == gsel
        keep = jnp.logical_or(keep, hit)
        cur = jnp.where(hit, neg, cur)
    cur = jnp.where(keep, sel, neg)
    idx_out = jnp.zeros(scores.shape, jnp.int32)
    w_out = jnp.zeros(scores.shape, F32)
    onehots = []
    for k in range(TOP_K):
        m = rmax(cur)
        ik = rmin(jnp.where(cur == m, lane, big))
        hit = lane == ik
        sk = jnp.sum(jnp.where(hit, scores, 0.0), axis=1, keepdims=True)
        idx_out = jnp.where(lane == k, ik, idx_out)
        w_out = jnp.where(lane == k, sk, w_out)
        onehots.append(hit)
        cur = jnp.where(hit, neg, cur)
    w_out = w_out * (ROUTED_SCALE / jnp.sum(w_out, axis=1, keepdims=True))
    chosen = onehots[0]
    for oh in onehots[1:]:
        chosen = jnp.logical_or(chosen, oh)
    e_mat = jnp.where(chosen, 1.0, 0.0)
    r_i = lax.broadcasted_iota(jnp.int32, (tm, tm), 0)
    c_i = lax.broadcasted_iota(jnp.int32, (tm, tm), 1)
    lower = jnp.where(c_i < r_i, 1.0, 0.0).astype(BF16)
    before = _dot(lower, e_mat.astype(BF16)) + carry_ref[...]
    rank_out = jnp.zeros(scores.shape, jnp.int32)
    for k in range(TOP_K):
        rk = jnp.sum(jnp.where(onehots[k], before, 0.0), axis=1, keepdims=True)
        rank_out = jnp.where(lane == k, rk.astype(jnp.int32), rank_out)
    carry_ref[...] = carry_ref[...] + jnp.sum(e_mat, axis=0, keepdims=True)
    idx_ref[...] = idx_out
    wt_ref[...] = w_out
    rank_ref[...] = rank_out
    cnt_ref[...] = jnp.broadcast_to(carry_ref[...], cnt_ref.shape).astype(jnp.int32)


def _router(h, w_router, bias):
    n, d = h.shape
    tm = _pick(n, 256, SUBLANES)
    w = jnp.pad(w_router, ((0, 0), (0, LANES - N_EXPERTS))).astype(BF16)
    bz = jnp.pad(bias.astype(F32).reshape(1, -1), ((0, 0), (0, LANES - N_EXPERTS)))
    tile = pl.BlockSpec((tm, LANES), lambda i: (i, 0))
    return pl.pallas_call(
        _router_kernel,
        grid=(n // tm,),
        in_specs=[
            pl.BlockSpec((tm, d), lambda i: (i, 0)),
            pl.BlockSpec((d, LANES), lambda i: (0, 0)),
            pl.BlockSpec((1, LANES), lambda i: (0, 0)),
        ],
        out_specs=[tile, tile, tile, pl.BlockSpec((SUBLANES, LANES), lambda i: (0, 0))],
        out_shape=[
            jax.ShapeDtypeStruct((n, LANES), jnp.int32),
            jax.ShapeDtypeStruct((n, LANES), F32),
            jax.ShapeDtypeStruct((n, LANES), jnp.int32),
            jax.ShapeDtypeStruct((SUBLANES, LANES), jnp.int32),
        ],
        scratch_shapes=[pltpu.VMEM((1, LANES), F32)],
        compiler_params=_cparams(("arbitrary",)),
        name="router",
    )(h, w, bz)


def _expert_kernel(be_ref, nu_ref, tok_ref, nxt_ref, h_hbm, wg_ref, wu_ref, wd_ref, y_ref,
                   xbuf, wgb, wub, wdb, sem):
    i = pl.program_id(0)
    n_used = nu_ref[0]
    slot = i % 2

    def row_copy(tok, slot_, r):
        return pltpu.make_async_copy(h_hbm.at[pl.ds(tok, 1)], xbuf.at[slot_, pl.ds(r, 1)], sem.at[slot_])

    def start_gather(idx_ref, slot_):
        for r in range(MOE_ROWS):
            row_copy(idx_ref[0, 0, r], slot_, r).start()

    def wait_gather(slot_):
        pltpu.make_async_copy(h_hbm.at[pl.ds(0, MOE_ROWS)], xbuf.at[slot_], sem.at[slot_]).wait()

    @pl.when(i == 0)
    def _():
        start_gather(tok_ref, 0)

    @pl.when(i <= n_used)
    def _():
        wait_gather(slot)

    @pl.when(i < n_used)
    def _():
        @pl.when(jnp.logical_or(i == 0, be_ref[i] != be_ref[jnp.maximum(i - 1, 0)]))
        def _():
            wgb[...] = wg_ref[0, 0].astype(BF16)
            wub[...] = wu_ref[0, 0].astype(BF16)
            wdb[...] = wd_ref[0, 0].astype(BF16)

        x = xbuf[slot].astype(BF16)
        start_gather(nxt_ref, 1 - slot)
        a = _silu(_dot(x, wgb[...])) * _dot(x, wub[...])
        y_ref[...] = _dot(a.astype(BF16), wdb[...])

    @pl.when(i >= n_used)
    def _():
        y_ref[...] = jnp.zeros_like(y_ref)

    @pl.when(jnp.logical_and(i == pl.num_programs(0) - 1, i < n_used))
    def _():
        wait_gather(1 - slot)


def _routed_experts(h, row_tok, block_e, n_used, wg, wu, wd, layer):
    n, d = h.shape
    n_blocks = block_e.shape[0]
    ff = wg.shape[3]
    tok3 = row_tok.reshape(n_blocks, 1, MOE_ROWS)
    grid_spec = pltpu.PrefetchScalarGridSpec(
        num_scalar_prefetch=2,
        grid=(n_blocks,),
        in_specs=[
            pl.BlockSpec((1, 1, MOE_ROWS), lambda i, be, nu: (i, 0, 0), memory_space=pltpu.SMEM),
            pl.BlockSpec((1, 1, MOE_ROWS), lambda i, be, nu: (jnp.minimum(i + 1, n_blocks - 1), 0, 0),
                         memory_space=pltpu.SMEM),
            pl.BlockSpec(memory_space=pl.ANY),
            pl.BlockSpec((1, 1, d, ff), lambda i, be, nu: (layer, be[i], 0, 0)),
            pl.BlockSpec((1, 1, d, ff), lambda i, be, nu: (layer, be[i], 0, 0)),
            pl.BlockSpec((1, 1, ff, d), lambda i, be, nu: (layer, be[i], 0, 0)),
        ],
        out_specs=pl.BlockSpec((MOE_ROWS, d), lambda i, be, nu: (i, 0)),
        scratch_shapes=[pltpu.VMEM((2, MOE_ROWS, d), F32),
                        pltpu.VMEM((d, ff), BF16), pltpu.VMEM((d, ff), BF16), pltpu.VMEM((ff, d), BF16),
                        pltpu.SemaphoreType.DMA((2,))],
    )
    return pl.pallas_call(
        _expert_kernel,
        grid_spec=grid_spec,
        out_shape=jax.ShapeDtypeStruct((n_blocks * MOE_ROWS, d), F32),
        compiler_params=_cparams(("arbitrary",)),
        name="routed_experts",
    )(block_e, n_used, tok3, tok3, h, wg, wu, wd)


def _shared_expert_kernel(h_ref, wsg_ref, wsu_ref, wsd_ref, o_ref):
    hb = h_ref[...].astype(BF16)
    a = _silu(_dot(hb, wsg_ref[...])) * _dot(hb, wsu_ref[...])
    o_ref[...] = _dot(a.astype(BF16), wsd_ref[...])


def _shared_expert(h, wsg, wsu, wsd):
    n, d = h.shape
    ff = wsg.shape[1]
    tm = _pick(n, 512, SUBLANES)
    tile = pl.BlockSpec((tm, d), lambda i: (i, 0))
    return pl.pallas_call(
        _shared_expert_kernel,
        grid=(n // tm,),
        in_specs=[tile,
                  pl.BlockSpec((d, ff), lambda i: (0, 0)),
                  pl.BlockSpec((d, ff), lambda i: (0, 0)),
                  pl.BlockSpec((ff, d), lambda i: (0, 0))],
        out_specs=tile,
        out_shape=jax.ShapeDtypeStruct((n, d), F32),
        compiler_params=_cparams(("parallel",)),
        name="shared_expert",
    )(h, wsg, wsu, wsd)


def _combine_kernel(pos_ref, nxt_ref, y_hbm, wt_ref, sh_ref, x_ref, g_ref, o_ref, ybuf, sem, *, tt):
    i = pl.program_id(0)
    n = pl.num_programs(0)
    slot = i % 2
    rows = tt * TOP_K

    def row_copy(src, slot_, r):
        return pltpu.make_async_copy(y_hbm.at[pl.ds(src, 1)], ybuf.at[slot_, pl.ds(r, 1)], sem.at[slot_])

    def start_gather(idx_ref, slot_):
        for r in range(rows):
            row_copy(idx_ref[0, 0, r], slot_, r).start()

    def wait_gather(slot_):
        pltpu.make_async_copy(y_hbm.at[pl.ds(0, rows)], ybuf.at[slot_], sem.at[slot_]).wait()

    @pl.when(i == 0)
    def _():
        start_gather(pos_ref, 0)

    wait_gather(slot)
    start_gather(nxt_ref, 1 - slot)
    wt = wt_ref[...]
    cw = min(COMBINE_COLS, x_ref.shape[1])
    for c0 in range(0, x_ref.shape[1], cw):
        cs = slice(c0, c0 + cw)
        acc = sh_ref[:, cs]
        for k in range(TOP_K):
            acc = acc + wt[:, k:k + 1] * ybuf[slot, k * tt:(k + 1) * tt, cs]
        o_ref[:, cs] = x_ref[:, cs] + g_ref[0, :, cs] * acc

    @pl.when(i == n - 1)
    def _():
        wait_gather(1 - slot)


def _combine(y, pos, wts, shared, x, mod_rows, layer, batch, n_ctx):
    n, d = x.shape
    t = n // batch
    tt = _pick(n_ctx, 64, SUBLANES)
    n_tiles = n // tt
    tiles_per_b = t // tt
    nct = n_ctx // tt
    pos3 = pos.reshape(n_tiles, tt, TOP_K).transpose(0, 2, 1).reshape(n_tiles, 1, tt * TOP_K)
    row = lambda i: (layer * SUBLANES + jnp.where(i % tiles_per_b < nct, batch, i // tiles_per_b)) * N_MOD + 5
    kern = functools.partial(_combine_kernel, tt=tt)
    tile = pl.BlockSpec((tt, d), lambda i: (i, 0))
    return pl.pallas_call(
        kern,
        grid=(n_tiles,),
        in_specs=[
            pl.BlockSpec((1, 1, tt * TOP_K), lambda i: (i, 0, 0), memory_space=pltpu.SMEM),
            pl.BlockSpec((1, 1, tt * TOP_K), lambda i: (jnp.minimum(i + 1, n_tiles - 1), 0, 0),
                         memory_space=pltpu.SMEM),
            pl.BlockSpec(memory_space=pl.ANY),
            pl.BlockSpec((tt, LANES), lambda i: (i, 0)),
            tile, tile,
            pl.BlockSpec((1, 1, d), lambda i: (row(i), 0, 0)),
        ],
        out_specs=tile,
        out_shape=jax.ShapeDtypeStruct((n, d), F32),
        scratch_shapes=[pltpu.VMEM((2, tt * TOP_K, d), F32), pltpu.SemaphoreType.DMA((2,))],
        compiler_params=_cparams(("arbitrary",)),
        name="moe_combine",
    )(pos3, pos3, y, wts, shared, x, mod_rows)


def _moe(h, x, mod_rows, layer, w_router, router_bias, wg, wu, wd, wsg, wsu, wsd, batch, n_ctx):
    n, d = h.shape
    idx, wts, rank, counts = _router(h, w_router, router_bias)
    idx = idx[:, :TOP_K]
    rank = rank[:, :TOP_K]
    counts = counts[0, :N_EXPERTS]
    padded = (counts + MOE_ROWS - 1) // MOE_ROWS * MOE_ROWS
    pad_end = jnp.cumsum(padded)
    pad_start = pad_end - padded
    pos = pad_start[idx] + rank
    n_blocks = -(-(n * TOP_K) // MOE_ROWS) + N_EXPERTS
    tok = jnp.broadcast_to(jnp.arange(n, dtype=jnp.int32)[:, None], pos.shape)
    row_tok = jnp.zeros((n_blocks * MOE_ROWS,), jnp.int32).at[pos.reshape(-1)].set(tok.reshape(-1))
    block_start = jnp.arange(n_blocks, dtype=jnp.int32) * MOE_ROWS
    block_e = jnp.minimum(jnp.sum(pad_end[None, :] <= block_start[:, None], axis=1), N_EXPERTS - 1).astype(jnp.int32)
    n_used = (pad_end[-1] // MOE_ROWS).astype(jnp.int32).reshape(1)
    y = _routed_experts(h, row_tok, block_e, n_used, wg, wu, wd, layer)
    shared = _shared_expert(h, wsg, wsu, wsd)
    return _combine(y, pos.astype(jnp.int32), wts, shared, x, mod_rows, layer, batch, n_ctx)


def _rope_tables(n_ctx, n_lat):
    rows = n_lat // GRID_W
    row, col = jnp.meshgrid(jnp.arange(rows), jnp.arange(GRID_W), indexing="ij")
    half = HEAD_DIM // 2
    inv_freq = ROPE_THETA ** (-jnp.arange(0, half, 2, dtype=F32) / half)
    ang_r = row.reshape(-1, 1).astype(F32) * inv_freq
    ang_c = col.reshape(-1, 1).astype(F32) * inv_freq
    ang = jnp.concatenate([ang_r, ang_r, ang_c, ang_c], axis=-1)
    cos = jnp.concatenate([jnp.ones((n_ctx, HEAD_DIM), F32), jnp.cos(ang)], axis=0)
    sin = jnp.concatenate([jnp.zeros((n_ctx, HEAD_DIM), F32), jnp.sin(ang)], axis=0)
    first = (jnp.arange(HEAD_DIM) % 64) < 32
    return cos, jnp.where(first[None, :], -sin, sin)


def kernel(x, c, ctx, c_ctx, w_mod, b_mod, norm1_g, norm2_g, w_in, w_out, dn_conv_w, dn_a_log, dn_dt_bias, dn_norm_g, hg_lb_logits, hg_norm_g, att_q_norm_g, att_k_norm_g, w_router, router_bias, w_exp_gate, w_exp_up, w_exp_down, w_sh_gate, w_sh_up, w_sh_down, final_norm_g):
    bsz, n_lat, d = x.shape
    n_ctx = ctx.shape[1]
    depth = w_in.shape[0]
    t = n_ctx + n_lat
    assert bsz + 1 <= SUBLANES and n_ctx % CHUNK == 0 and n_lat % CHUNK == 0 and n_lat % GRID_W == 0

    cos, sin_s = _rope_tables(n_ctx, n_lat)
    lb_p = jax.nn.softmax(hg_lb_logits.astype(F32), axis=0)
    lower_bounds = jnp.cumsum(lb_p, axis=0) - lb_p[0]

    c_all = jnp.concatenate([c, c_ctx[None, :], jnp.zeros((SUBLANES - bsz - 1, d), c.dtype)], axis=0)
    mod = _mod_vectors(c_all, w_mod, b_mod)
    mod_rows = mod.reshape(depth * SUBLANES * N_MOD, 1, d)

    xs = jnp.concatenate([ctx, x], axis=1)
    for l in range(depth):
        h1 = _norm_mod(xs, norm1_g, mod_rows, l, 0, n_ctx, BF16)
        p = _matmul(h1.reshape(bsz * t, d), _permute_w_in(w_in[l]), F32).reshape(bsz, t, P_WIDTH)
        qr, kr, vr = _att_prep(p, cos, sin_s, att_q_norm_g[l], att_k_norm_g[l])
        og = _attention(qr, kr, vr, n_ctx)
        hf, hb = _hgrn2(p, lower_bounds[l], n_ctx)
        ob = _mix_finish(hf, hb, p, C_HG, hg_norm_g[l])
        dq, dk, dv, gcol, gtot, gcr = _dn_prep(p, dn_conv_w[l], dn_a_log[l], dn_dt_bias[l], n_ctx)
        u, w, qd, kd, qk = _dn_intra(dq, dk, dv, gcol, gtot, gcr)
        df, db = _dn_scan(u, w, qd, kd, qk, gtot, n_ctx)
        oa = _mix_finish(df, db, p, C_DZ, dn_norm_g[l])
        xs = _out_proj(oa, ob, og, w_out[l], xs, mod_rows, l, n_ctx)
        h2 = _norm_mod(xs, norm2_g, mod_rows, l, 3, n_ctx, F32).reshape(bsz * t, d)
        xs = _moe(h2, xs.reshape(bsz * t, d), mod_rows, l, w_router[l], router_bias[l],
                  w_exp_gate, w_exp_up, w_exp_down,
                  w_sh_gate[l].astype(BF16), w_sh_up[l].astype(BF16), w_sh_down[l].astype(BF16),
                  bsz, n_ctx).reshape(bsz, t, d)
    return _final_norm(xs, final_norm_g, n_ctx)
```

```python
import functools

import numpy as np
import jax
import jax.numpy as jnp
from jax import lax
from jax.experimental import pallas as pl
from jax.experimental.pallas import tpu as pltpu

F32 = jnp.float32
BF16 = jnp.bfloat16

EPS = 1e-6
GATE_FLOOR = 1e-30
HEAD_DIM = 128
DN_HEADS = 8
DN_WIDTH = DN_HEADS * HEAD_DIM
DN_CONV = 5
HG_HEADS = 8
HG_WIDTH = HG_HEADS * HEAD_DIM
ATT_Q_HEADS = 16
ATT_KV_HEADS = 4
ATT_GROUP = ATT_Q_HEADS // ATT_KV_HEADS
ATT_Q_WIDTH = ATT_Q_HEADS * HEAD_DIM
ATT_KV_WIDTH = ATT_KV_HEADS * HEAD_DIM
GRID_W = 64
ROPE_THETA = 10000.0
CHUNK = 64
N_EXPERTS = 64
TOP_K = 8
N_GROUPS = 8
TOPK_GROUPS = 4
GROUP_SIZE = N_EXPERTS // N_GROUPS
ROUTED_SCALE = 2.5
N_MOD = 6
LOG2E = 1.4426950408889634

C_AQ = 0
C_AK = C_AQ + ATT_Q_WIDTH
C_AV = C_AK + ATT_KV_WIDTH
C_DQ = C_AV + ATT_KV_WIDTH
C_DZ = C_DQ + 3 * DN_WIDTH
C_HQ = C_DZ + DN_WIDTH
C_HF = C_HQ + HG_WIDTH
C_HI = C_HF + 2 * HG_WIDTH
C_HG = C_HI + HG_WIDTH
C_AB = C_HG + HG_WIDTH
LANES = 128
SUBLANES = 8
P_WIDTH = C_AB + 4 * LANES

MOE_ROWS = 256
EXPERT_BUFS = 3
COMBINE_COLS = 512
VMEM_LIMIT = 56 * 1024 * 1024


def _cparams(sem, vmem=VMEM_LIMIT):
    return pltpu.CompilerParams(dimension_semantics=sem, vmem_limit_bytes=vmem)


def _pick(n, cap, mult):
    if n <= cap:
        return n
    for t in range(cap - cap % mult, 0, -mult):
        if n % t == 0:
            return t
    raise ValueError(f"no tile for {n} (cap {cap}, multiple of {mult})")


def _silu(x):
    return x * (1.0 / (1.0 + jnp.exp(-x)))


def _sigmoid(x):
    return 1.0 / (1.0 + jnp.exp(-x))


def _dot(a, b):
    return jnp.dot(a, b, preferred_element_type=F32)


def _dot_nt(a, b):
    return lax.dot_general(a, b, (((1,), (1,)), ((), ())), preferred_element_type=F32)


def _dot_tn(a, b):
    return lax.dot_general(a, b, (((0,), (0,)), ((), ())), preferred_element_type=F32)


def _split2(a):
    hi = a.astype(BF16)
    lo = (a - hi.astype(F32)).astype(BF16)
    return hi, lo


_LEVELS = (32, 16, 8, 4, 2, 1)


@functools.lru_cache(maxsize=None)
def _dir_consts():
    c = CHUNK
    t = np.arange(c)
    incl = [(t[None, :] <= t[:, None]), (t[None, :] >= t[:, None])]
    cms, masks = [], []
    for d in range(2):
        md = incl[d].astype(np.float32)
        blocks, mks = [md], []
        for b in _LEVELS:
            blk = t // (2 * b)
            ref = blk * 2 * b + (b - 1 if d == 0 else b)
            blocks.append(md[ref, :])
            same = blk[:, None] == blk[None, :]
            late = (t % (2 * b)) >= b
            if d == 0:
                mks.append(same & late[:, None] & ~late[None, :])
            else:
                mks.append(same & ~late[:, None] & late[None, :])
        mks.append(np.eye(c, dtype=bool))
        blocks.append(np.ones((c, c), np.float32))
        cms.append(np.concatenate(blocks, 0))
        masks.append(np.stack(mks).astype(np.float32))
    cm = np.stack(cms)
    lv = np.stack(masks)
    incl_f = np.stack([m.astype(np.float32) for m in incl])
    bd16 = ((t[:, None] // 16) == (t[None, :] // 16)).astype(np.float32)
    return cm, lv, incl_f, bd16


def _mod_kernel(c_ref, w_ref, b_ref, o_ref):
    a = _silu(c_ref[...]).astype(BF16)
    o_ref[0] = _dot(a, w_ref[0].astype(BF16)) + b_ref[0]


def _mod_vectors(c_all, w_mod, b_mod):
    depth, d, n = w_mod.shape
    tn = _pick(n, 512, LANES)
    return pl.pallas_call(
        _mod_kernel,
        grid=(depth, n // tn),
        in_specs=[
            pl.BlockSpec((SUBLANES, d), lambda l, j: (0, 0)),
            pl.BlockSpec((1, d, tn), lambda l, j: (l, 0, j)),
            pl.BlockSpec((1, 1, tn), lambda l, j: (l, 0, j)),
        ],
        out_specs=pl.BlockSpec((1, SUBLANES, tn), lambda l, j: (l, 0, j)),
        out_shape=jax.ShapeDtypeStruct((depth, SUBLANES, n), F32),
        compiler_params=_cparams(("parallel", "parallel")),
        name="mod_vectors",
    )(c_all, w_mod, b_mod.reshape(depth, 1, n))


def _norm_mod_kernel(x_ref, g_ref, sh_ref, sc_ref, o_ref):
    x = x_ref[0]
    var = jnp.mean(x * x, axis=-1, keepdims=True)
    y = x * lax.rsqrt(var + EPS) * g_ref[0]
    o_ref[0] = (y * (1.0 + sc_ref[0]) + sh_ref[0]).astype(o_ref.dtype)


def _norm_mod(x, gain, mod_rows, layer, k_shift, n_ctx, out_dtype):
    b, t, d = x.shape
    tt = _pick(n_ctx, 256, SUBLANES)
    nct = n_ctx // tt

    def mrow(k):
        return lambda bi, j: ((layer * SUBLANES + jnp.where(j < nct, b, bi)) * N_MOD + k, 0, 0)

    return pl.pallas_call(
        _norm_mod_kernel,
        grid=(b, t // tt),
        in_specs=[
            pl.BlockSpec((1, tt, d), lambda bi, j: (bi, j, 0)),
            pl.BlockSpec((1, 1, d), lambda bi, j: (layer, 0, 0)),
            pl.BlockSpec((1, 1, d), mrow(k_shift)),
            pl.BlockSpec((1, 1, d), mrow(k_shift + 1)),
        ],
        out_specs=pl.BlockSpec((1, tt, d), lambda bi, j: (bi, j, 0)),
        out_shape=jax.ShapeDtypeStruct((b, t, d), out_dtype),
        compiler_params=_cparams(("parallel", "parallel")),
        name="norm_mod",
    )(x, gain.reshape(gain.shape[0], 1, d), mod_rows, mod_rows)


def _final_norm_kernel(x_ref, g_ref, o_ref):
    x = x_ref[0]
    var = jnp.mean(x * x, axis=-1, keepdims=True)
    o_ref[0] = x * lax.rsqrt(var + EPS) * g_ref[...]


def _final_norm(x, gain, n_ctx):
    b, t, d = x.shape
    n_lat = t - n_ctx
    tt = _pick(n_ctx, 256, SUBLANES)
    off = n_ctx // tt
    return pl.pallas_call(
        _final_norm_kernel,
        grid=(b, n_lat // tt),
        in_specs=[
            pl.BlockSpec((1, tt, d), lambda bi, j: (bi, j + off, 0)),
            pl.BlockSpec((1, d), lambda bi, j: (0, 0)),
        ],
        out_specs=pl.BlockSpec((1, tt, d), lambda bi, j: (bi, j, 0)),
        out_shape=jax.ShapeDtypeStruct((b, n_lat, d), F32),
        compiler_params=_cparams(("parallel", "parallel")),
        name="final_norm",
    )(x, gain.reshape(1, d))


def _mm_kernel(a_ref, w_ref, o_ref):
    o_ref[...] = _dot(a_ref[...], w_ref[...]).astype(o_ref.dtype)


def _matmul(a, w, out_dtype):
    m, k = a.shape
    n = w.shape[1]
    tm = _pick(m, 1536, 256) if m > 1536 else m
    tn = _pick(n, 512, LANES)
    return pl.pallas_call(
        _mm_kernel,
        grid=(m // tm, n // tn),
        in_specs=[
            pl.BlockSpec((tm, k), lambda i, j: (i, 0)),
            pl.BlockSpec((k, tn), lambda i, j: (0, j)),
        ],
        out_specs=pl.BlockSpec((tm, tn), lambda i, j: (i, j)),
        out_shape=jax.ShapeDtypeStruct((m, n), out_dtype),
        compiler_params=_cparams(("parallel", "parallel")),
        name="in_proj",
    )(a, w)


def _permute_w_in(w):
    d = w.shape[0]
    dn_cols = 4 * DN_WIDTH + 4 * DN_HEADS
    hg_cols = 5 * HG_WIDTH
    att0 = dn_cols + hg_cols
    parts = [
        w[:, att0:],
        w[:, :4 * DN_WIDTH],
        w[:, dn_cols:att0],
        w[:, 4 * DN_WIDTH:dn_cols],
        jnp.zeros((d, P_WIDTH - C_AB - 4 * DN_HEADS), w.dtype),
    ]
    return jnp.concatenate(parts, axis=1).astype(BF16)


def _att_prep_kernel(q_ref, k_ref, v_ref, cos_ref, sin_ref, qg_ref, kg_ref, qo_ref, ko_ref, vo_ref):
    cos = cos_ref[...]
    sin = sin_ref[...]
    lane = lax.broadcasted_iota(jnp.int32, cos.shape, 1)
    first = (lane % 64) < 32

    def norm_rope(x, g, scale):
        y = x * lax.rsqrt(jnp.mean(x * x, axis=-1, keepdims=True) + EPS) * g
        r = jnp.where(first, pltpu.roll(y, HEAD_DIM - 32, 1), pltpu.roll(y, 32, 1))
        return (y * cos + r * sin) * scale

    q = q_ref[0]
    qg = qg_ref[...]
    for h in range(ATT_Q_HEADS):
        sl = slice(h * HEAD_DIM, (h + 1) * HEAD_DIM)
        qo_ref[0, :, sl] = norm_rope(q[:, sl], qg, HEAD_DIM ** -0.5 * LOG2E).astype(qo_ref.dtype)
    k = k_ref[0]
    kg = kg_ref[...]
    for h in range(ATT_KV_HEADS):
        sl = slice(h * HEAD_DIM, (h + 1) * HEAD_DIM)
        ko_ref[0, :, sl] = norm_rope(k[:, sl], kg, 1.0).astype(ko_ref.dtype)
    vo_ref[0] = v_ref[0].astype(vo_ref.dtype)


def _att_prep(p, cos, sin_s, qg, kg):
    b, t, _ = p.shape
    tt = _pick(t, 256, SUBLANES)
    kvb = C_AK // ATT_KV_WIDTH
    return pl.pallas_call(
        _att_prep_kernel,
        grid=(b, t // tt),
        in_specs=[
            pl.BlockSpec((1, tt, ATT_Q_WIDTH), lambda bi, j: (bi, j, 0)),
            pl.BlockSpec((1, tt, ATT_KV_WIDTH), lambda bi, j: (bi, j, kvb)),
            pl.BlockSpec((1, tt, ATT_KV_WIDTH), lambda bi, j: (bi, j, kvb + 1)),
            pl.BlockSpec((tt, HEAD_DIM), lambda bi, j: (j, 0)),
            pl.BlockSpec((tt, HEAD_DIM), lambda bi, j: (j, 0)),
            pl.BlockSpec((1, HEAD_DIM), lambda bi, j: (0, 0)),
            pl.BlockSpec((1, HEAD_DIM), lambda bi, j: (0, 0)),
        ],
        out_specs=[
            pl.BlockSpec((1, tt, ATT_Q_WIDTH), lambda bi, j: (bi, j, 0)),
            pl.BlockSpec((1, tt, ATT_KV_WIDTH), lambda bi, j: (bi, j, 0)),
            pl.BlockSpec((1, tt, ATT_KV_WIDTH), lambda bi, j: (bi, j, 0)),
        ],
        out_shape=[
            jax.ShapeDtypeStruct((b, t, ATT_Q_WIDTH), BF16),
            jax.ShapeDtypeStruct((b, t, ATT_KV_WIDTH), BF16),
            jax.ShapeDtypeStruct((b, t, ATT_KV_WIDTH), BF16),
        ],
        compiler_params=_cparams(("parallel", "parallel")),
        name="att_prep",
    )(p, p, p, cos, sin_s, qg.reshape(1, HEAD_DIM), kg.reshape(1, HEAD_DIM))


ATT_ROW_BLOCK = 512
ATT_KEY_TILE = 256
ATT_KEY_CHUNK = 1024


def _attn_kernel(q_ref, k_ref, v_ref, o_ref, qs_ref, sa_ref, sb_ref, pa_ref, pb_ref, ala_ref, alb_ref,
                 m_ref, l_ref, acc_ref, *, tq, tk, n_ctx, lat_chunks, ctx_qtiles):
    i = pl.program_id(2)
    rows = ATT_GROUP * tq
    for g in range(ATT_GROUP):
        qs_ref[g * tq:(g + 1) * tq, :] = q_ref[0, :, g * HEAD_DIM:(g + 1) * HEAD_DIM]

    n_blocks = rows // ATT_ROW_BLOCK

    def max_pass(s_ref, al_ref, width, first, blocks):
        for r in blocks:
            rs = slice(r * ATT_ROW_BLOCK, (r + 1) * ATT_ROW_BLOCK)
            mx = s_ref[rs, 0:LANES]
            for c0 in range(LANES, width, LANES):
                mx = jnp.maximum(mx, s_ref[rs, c0:c0 + LANES])
            m_new = jnp.max(mx, axis=1, keepdims=True)
            if first:
                al_ref[rs, :] = jnp.ones_like(m_new)
            else:
                m_old = m_ref[rs, :]
                m_new = jnp.maximum(m_old, m_new)
                al_ref[rs, :] = jnp.exp2(m_old - m_new)
            m_ref[rs, :] = m_new

    def exp_pass(s_ref, p_ref, al_ref, width, first, blocks):
        for r in blocks:
            rs = slice(r * ATT_ROW_BLOCK, (r + 1) * ATT_ROW_BLOCK)
            m_new = m_ref[rs, :]
            psum = jnp.zeros((ATT_ROW_BLOCK, LANES), F32)
            for c0 in range(0, width, LANES):
                pr = jnp.exp2(s_ref[rs, c0:c0 + LANES] - m_new)
                psum = psum + pr
                p_ref[rs, c0:c0 + LANES] = pr.astype(p_ref.dtype)
            lsum = jnp.sum(psum, axis=1, keepdims=True)
            if first:
                l_ref[rs, :] = lsum
            else:
                l_ref[rs, :] = al_ref[rs, :] * l_ref[rs, :] + lsum

    def softmax_update(s_ref, p_ref, al_ref, width, first):
        max_pass(s_ref, al_ref, width, first, range(n_blocks))
        exp_pass(s_ref, p_ref, al_ref, width, first, range(n_blocks))

    def write_out():
        out = acc_ref[...] * (1.0 / l_ref[...])
        for g in range(ATT_GROUP):
            o_ref[0, :, g * HEAD_DIM:(g + 1) * HEAD_DIM] = out[g * tq:(g + 1) * tq].astype(o_ref.dtype)

    sb_ref[:, :n_ctx] = _dot_nt(qs_ref[...], k_ref[0, 0:n_ctx, :])

    @pl.when(i < ctx_qtiles)
    def _():
        softmax_update(sb_ref, pb_ref, alb_ref, n_ctx, True)
        acc_ref[...] = _dot(pb_ref[:, :n_ctx], v_ref[0, 0:n_ctx, :])
        write_out()

    @pl.when(i >= ctx_qtiles)
    def _():
        def rows_at(ref, start):
            if isinstance(start, int):
                return ref[0, start:start + tk, :]
            return ref[0, pl.ds(pl.multiple_of(start, LANES), tk), :]

        def k_chunk(c):
            return rows_at(k_ref, n_ctx + c * tk)

        def v_before(c):
            start = n_ctx + (c - 1) * tk
            return rows_at(v_ref, max(start, 0) if isinstance(start, int) else jnp.maximum(start, 0))

        def pv(p_ref, al_ref, vc):
            acc_ref[...] = al_ref[...] * acc_ref[...] + _dot(p_ref[...], vc)

        def stage(c, s_cur, p_cur, al_cur, s_nxt, p_prev, al_prev):
            pv(p_prev, al_prev, v_before(c))
            max_pass(s_cur, al_cur, tk, False, range(n_blocks))
            kt = min(ATT_KEY_TILE, tk)
            parts = tk // kt
            k_next = k_chunk(c + 1) if s_nxt is not None else None
            for j in range(parts):
                if s_nxt is not None:
                    ks = slice(j * kt, (j + 1) * kt)
                    s_nxt[:, ks] = _dot_nt(qs_ref[...], k_next[ks])
                exp_pass(s_cur, p_cur, al_cur, tk, False,
                         range(j * n_blocks // parts, (j + 1) * n_blocks // parts))

        a_bufs = (sa_ref, pa_ref, ala_ref)
        b_bufs = (sb_ref, pb_ref, alb_ref)
        if n_ctx < tk:
            pb_ref[:, n_ctx:] = jnp.zeros((rows, tk - n_ctx), pb_ref.dtype)
        acc_ref[...] = jnp.zeros_like(acc_ref)
        sa_ref[...] = _dot_nt(qs_ref[...], k_chunk(0))
        softmax_update(sb_ref, pb_ref, alb_ref, n_ctx, True)

        def pair(cp, carry):
            c0 = 2 * cp
            stage(c0, *a_bufs, sb_ref, pb_ref, alb_ref)
            stage(c0 + 1, *b_bufs, sa_ref, pa_ref, ala_ref)
            return carry

        lax.fori_loop(0, lat_chunks // 2 - 1, pair, 0)
        stage(lat_chunks - 2, *a_bufs, sb_ref, pb_ref, alb_ref)
        stage(lat_chunks - 1, *b_bufs, None, pa_ref, ala_ref)
        pv(pb_ref, alb_ref, v_before(lat_chunks))
        write_out()


def _attention(qr, kr, vr, n_ctx):
    b, t, _ = qr.shape
    n_lat = t - n_ctx
    tq = _pick(n_ctx, 256, SUBLANES)
    tk = _pick(n_lat // 2, ATT_KEY_CHUNK, LANES)
    assert n_lat % (2 * tk) == 0 and n_ctx % LANES == 0 and n_ctx <= tk
    rows = ATT_GROUP * tq
    gw = ATT_GROUP * HEAD_DIM
    kern = functools.partial(_attn_kernel, tq=tq, tk=tk, n_ctx=n_ctx, lat_chunks=n_lat // tk,
                             ctx_qtiles=n_ctx // tq)
    return pl.pallas_call(
        kern,
        grid=(b, ATT_KV_HEADS, t // tq),
        in_specs=[
            pl.BlockSpec((1, tq, gw), lambda bi, h, i: (bi, i, h)),
            pl.BlockSpec((1, t, HEAD_DIM), lambda bi, h, i: (bi, 0, h)),
            pl.BlockSpec((1, t, HEAD_DIM), lambda bi, h, i: (bi, 0, h)),
        ],
        out_specs=pl.BlockSpec((1, tq, gw), lambda bi, h, i: (bi, i, h)),
        out_shape=jax.ShapeDtypeStruct((b, t, ATT_Q_WIDTH), BF16),
        scratch_shapes=[
            pltpu.VMEM((rows, HEAD_DIM), BF16),
            pltpu.VMEM((rows, tk), F32), pltpu.VMEM((rows, tk), F32),
            pltpu.VMEM((rows, tk), BF16), pltpu.VMEM((rows, tk), BF16),
            pltpu.VMEM((rows, 1), F32), pltpu.VMEM((rows, 1), F32),
            pltpu.VMEM((rows, 1), F32), pltpu.VMEM((rows, 1), F32),
            pltpu.VMEM((rows, HEAD_DIM), F32),
        ],
        compiler_params=_cparams(("parallel", "parallel", "parallel")),
        name="attention",
    )(qr, kr, vr)


def _chunk_maps(n_ctx, t, size=CHUNK):
    ncc = n_ctx // size
    nc = t // size

    def fwd(i):
        return i

    def bwd(i):
        return jnp.where(i < ncc, ncc - 1 - i, nc - 1 + ncc - i)

    return fwd, bwd, nc


def _bdot_tn(a, b):
    return lax.dot_general(a, b, (((1,), (1,)), ((0,), (0,))), preferred_element_type=F32)


def _hg_kernel(qf_ref, ff_ref, if_ref, qb_ref, fb_ref, ib_ref, lbf_ref, lbb_ref, cm_ref, lv_ref,
               of_ref, ob_ref, s_ref, *, nch):
    @pl.when(pl.program_id(2) == 0)
    def _():
        s_ref[...] = jnp.zeros_like(s_ref)

    c = CHUNK
    split = lambda a: a.reshape(nch, c, HEAD_DIM)

    def gates(q_raw, f_raw, lb):
        one_m_lb = 1.0 - lb
        f_gate = lb + one_m_lb * _sigmoid(f_raw)
        log_f = jnp.log(jnp.maximum(f_gate, GATE_FLOOR))
        return split(_silu(q_raw)), split(one_m_lb * _sigmoid(-f_raw)), split(log_f)

    parts = [gates(qf_ref[0], ff_ref[0], lbf_ref[0]), gates(qb_ref[0], fb_ref[0], lbb_ref[0])]
    q, key, log_f = (jnp.concatenate([parts[0][n], parts[1][n]], axis=0) for n in range(3))
    vb = jnp.concatenate([split(if_ref[0]), split(ib_ref[0])], axis=0).astype(BF16)
    hi = log_f.astype(BF16)
    r1 = log_f - hi.astype(F32)
    mid = r1.astype(BF16)
    lo = (r1 - mid.astype(F32)).astype(BF16)
    cm = cm_ref[...]
    x = _bdot(cm, hi) + _bdot(cm, mid) + _bdot(cm, lo)
    cum = x[:, 0:c]
    tot = x[:, 7 * c:7 * c + 1]
    lv = lv_ref[...]
    att = jnp.where(lv[:, 6] > 0, _bdot_nt(q.astype(BF16), key.astype(BF16)), 0.0)
    for n in range(len(_LEVELS)):
        ref = x[:, (n + 1) * c:(n + 2) * c]
        qs = (q * jnp.exp(jnp.minimum(cum - ref, 0.0))).astype(BF16)
        ks = (key * jnp.exp(jnp.minimum(ref - cum, 0.0))).astype(BF16)
        att = att + jnp.where(lv[:, n] > 0, _bdot_nt(qs, ks), 0.0)
    o_intra = _bdot(att.astype(BF16), vb)
    q_dec = (q * jnp.exp(cum)).astype(BF16)
    k_dec = (key * jnp.exp(tot - cum)).astype(BF16)
    kv = _bdot_tn(vb, k_dec)
    decay = jnp.exp(tot)
    for d, o_ref in ((0, of_ref), (1, ob_ref)):
        st = s_ref[d]
        for ci in (range(nch) if d == 0 else range(nch - 1, -1, -1)):
            n = d * nch + ci
            o_ref[0, ci * c:(ci + 1) * c, :] = o_intra[n] + _dot_nt(q_dec[n], st.astype(BF16))
            st = st * decay[n] + kv[n]
        s_ref[d] = st


def _hgrn2(p, lower_bound, n_ctx):
    b, t, _ = p.shape
    nch = min(4, n_ctx // CHUNK)
    tb = nch * CHUNK
    fwd, bwd, nblk = _chunk_maps(n_ctx, t, tb)
    cm, lv, _, _ = _dir_consts()
    cm = jnp.asarray(np.repeat(cm, nch, axis=0), BF16)
    lv = jnp.asarray(np.repeat(lv, nch, axis=0), F32)
    lb = lower_bound.astype(F32).reshape(2 * HG_HEADS, 1, HEAD_DIM)
    hq, hf, hi_ = C_HQ // HEAD_DIM, C_HF // HEAD_DIM, C_HI // HEAD_DIM
    blk = (1, tb, HEAD_DIM)

    def col(base, order):
        return pl.BlockSpec(blk, lambda bi, h, i: (bi, order(i), base + h))

    full = lambda shape: pl.BlockSpec(shape, lambda bi, h, i: (0,) * len(shape))
    return pl.pallas_call(
        functools.partial(_hg_kernel, nch=nch),
        grid=(b, HG_HEADS, nblk),
        in_specs=[
            col(hq, fwd), col(hf, fwd), col(hi_, fwd),
            col(hq, bwd), col(hf + HG_HEADS, bwd), col(hi_, bwd),
            pl.BlockSpec((1, 1, HEAD_DIM), lambda bi, h, i: (h, 0, 0)),
            pl.BlockSpec((1, 1, HEAD_DIM), lambda bi, h, i: (HG_HEADS + h, 0, 0)),
            full(cm.shape), full(lv.shape),
        ],
        out_specs=[
            pl.BlockSpec(blk, lambda bi, h, i: (bi, fwd(i), h)),
            pl.BlockSpec(blk, lambda bi, h, i: (bi, bwd(i), h)),
        ],
        out_shape=[jax.ShapeDtypeStruct((b, t, HG_WIDTH), F32)] * 2,
        scratch_shapes=[pltpu.VMEM((2, HEAD_DIM, HEAD_DIM), F32)],
        compiler_params=_cparams(("parallel", "parallel", "arbitrary")),
        name="hgrn2",
    )(p, p, p, p, p, p, lb, lb, cm, lv)


def _dn_prep_kernel(x_ref, prev_ref, next_ref, ab_ref, w_ref, alog_ref, dt_ref, cs_ref,
                    q_ref, k_ref, v_ref, gcol_ref, gtot_ref, gct_ref, *, tt, ctx_tiles, n_tiles):
    j = pl.program_id(1)
    seg_start = jnp.logical_or(j == 0, j == ctx_tiles)
    seg_end = jnp.logical_or(j == ctx_tiles - 1, j == n_tiles - 1)
    x = x_ref[0]
    ph = jnp.where(seg_start, 0.0, prev_ref[0])
    nh = jnp.where(seg_end, 0.0, next_ref[0])
    w = w_ref[0]
    r8 = lax.broadcasted_iota(jnp.int32, (SUBLANES, x.shape[1]), 0)
    half = DN_CONV // 2

    def later(s):
        r = pltpu.roll(x, tt - s, 0)
        tail = jnp.where(r8 >= SUBLANES - s, pltpu.roll(nh, SUBLANES - s, 0), r[tt - SUBLANES:])
        return jnp.concatenate([r[:tt - SUBLANES], tail], axis=0)

    def earlier(s):
        r = pltpu.roll(x, s, 0)
        head = jnp.where(r8 < s, pltpu.roll(ph, s, 0), r[:SUBLANES])
        return jnp.concatenate([head, r[SUBLANES:]], axis=0)

    acc = x * w[half:half + 1]
    for s in range(1, half + 1):
        acc = acc + later(s) * w[half + s:half + s + 1] + earlier(s) * w[half - s:half - s + 1]
    y = _silu(acc)
    for h in range(DN_HEADS):
        qh = y[:, h * HEAD_DIM:(h + 1) * HEAD_DIM]
        kh = y[:, DN_WIDTH + h * HEAD_DIM:DN_WIDTH + (h + 1) * HEAD_DIM]
        q_ref[0, h] = qh * lax.rsqrt(jnp.sum(qh * qh, axis=-1, keepdims=True) + EPS) * HEAD_DIM ** -0.5
        k_ref[0, h] = kh * lax.rsqrt(jnp.sum(kh * kh, axis=-1, keepdims=True) + EPS)
        v_ref[0, h] = y[:, 2 * DN_WIDTH + h * HEAD_DIM:2 * DN_WIDTH + (h + 1) * HEAD_DIM]
    a = ab_ref[0]
    z = a + dt_ref[...]
    softplus = jnp.maximum(z, 0.0) + jnp.log1p(jnp.exp(-jnp.abs(z)))
    lane = lax.broadcasted_iota(jnp.int32, a.shape, 1)
    gb = jnp.where(lane < 2 * DN_HEADS, -jnp.exp(alog_ref[...]) * softplus, _sigmoid(a))
    hi = gb.astype(BF16)
    r1 = gb - hi.astype(F32)
    mid = r1.astype(BF16)
    lo = (r1 - mid.astype(F32)).astype(BF16)
    cmat = cs_ref[...]
    xs = _dot(cmat, hi) + _dot(cmat, mid) + _dot(cmat, lo)
    gcol = jnp.where(lane < DN_HEADS, xs[:tt], jnp.where(lane < 2 * DN_HEADS, xs[tt:2 * tt], gb))
    gcol_ref[0] = gcol
    gtot_ref[0] = xs[2 * tt:]
    gct_ref[0] = gcol.T


def _dn_prep(p, conv_w, a_log, dt_bias, n_ctx):
    b, t, _ = p.shape
    tt = _pick(n_ctx, 128, LANES)
    n_tiles = t // tt
    r = tt // SUBLANES
    last8 = t // SUBLANES - 1
    cw = 3 * DN_WIDTH
    cb = C_DQ // cw
    pad = lambda v: jnp.pad(v.astype(F32).reshape(1, -1), ((0, 0), (0, LANES - 2 * DN_HEADS)))
    kern = functools.partial(_dn_prep_kernel, tt=tt, ctx_tiles=n_ctx // tt, n_tiles=n_tiles)
    hshape = jax.ShapeDtypeStruct((b, DN_HEADS, t, HEAD_DIM), F32)
    hspec = pl.BlockSpec((1, DN_HEADS, tt, HEAD_DIM), lambda bi, j: (bi, 0, j, 0))
    _, _, incl, _ = _dir_consts()
    eye = np.eye(tt // CHUNK, dtype=np.float32)
    cs = np.concatenate([np.kron(eye, incl[0]), np.kron(eye, incl[1]),
                         np.kron(eye, np.ones((CHUNK, CHUNK), np.float32))], axis=0)
    tile = pl.BlockSpec((1, tt, LANES), lambda bi, j: (bi, j, 0))
    q, k, v, gcol, gtot, gct = pl.pallas_call(
        kern,
        grid=(b, n_tiles),
        in_specs=[
            pl.BlockSpec((1, tt, cw), lambda bi, j: (bi, j, cb)),
            pl.BlockSpec((1, SUBLANES, cw), lambda bi, j: (bi, jnp.maximum(j * r - 1, 0), cb)),
            pl.BlockSpec((1, SUBLANES, cw), lambda bi, j: (bi, jnp.minimum((j + 1) * r, last8), cb)),
            pl.BlockSpec((1, tt, LANES), lambda bi, j: (bi, j, C_AB // LANES)),
            pl.BlockSpec((1, DN_CONV, cw), lambda bi, j: (0, 0, 0)),
            pl.BlockSpec((1, LANES), lambda bi, j: (0, 0)),
            pl.BlockSpec((1, LANES), lambda bi, j: (0, 0)),
            pl.BlockSpec((3 * tt, tt), lambda bi, j: (0, 0)),
        ],
        out_specs=[hspec, hspec, hspec, tile, tile,
                   pl.BlockSpec((1, LANES, tt), lambda bi, j: (bi, 0, j))],
        out_shape=[hshape, hshape, hshape,
                   jax.ShapeDtypeStruct((b, t, LANES), F32),
                   jax.ShapeDtypeStruct((b, t, LANES), F32),
                   jax.ShapeDtypeStruct((b, LANES, t), F32)],
        compiler_params=_cparams(("parallel", "parallel")),
        name="dn_prep",
    )(p, p, p, p, conv_w.astype(F32)[None], pad(a_log), pad(dt_bias), jnp.asarray(cs, BF16))
    gcr = gct[:, :2 * DN_HEADS].reshape(b, 2 * DN_HEADS, t // CHUNK, CHUNK).transpose(0, 2, 1, 3)
    return q, k, v, gcol, gtot, gcr


def _bdot(a, b):
    return lax.dot_general(a, b, (((2,), (1,)), ((0,), (0,))), preferred_element_type=F32)


def _bdot_nt(a, b):
    return lax.dot_general(a, b, (((2,), (2,)), ((0,), (0,))), preferred_element_type=F32)


def _bdot3(a, b):
    ah, al = _split2(a)
    bh, bl = _split2(b)
    return _bdot(ah, bh) + _bdot(ah, bl) + _bdot(al, bh)


def _tri_inverse(a, eye, bd16, off32, off64):
    n = -(a * bd16)
    s = eye + n
    p = n
    for _ in range(3):
        p = _bdot3(p, p)
        s = s + _bdot3(p, s)
    for off in (off32, off64):
        s = s - _bdot3(_bdot3(s, a * off), s)
    return s


def _dn_intra_kernel(q_ref, k_ref, v_ref, gcol_ref, gtot_ref, gcr_ref, incl_ref, off32_ref, off64_ref, bd_ref,
                     u_ref, w_ref, qd_ref, kd_ref, qk_ref, *, nch):
    h = pl.program_id(1)
    c = CHUNK
    tb = nch * c
    split = lambda a: a.reshape(nch, c, a.shape[-1])
    both = lambda a: jnp.concatenate([a, a], axis=0)
    q3, k3, v3 = split(q_ref[0, 0]), split(k_ref[0, 0]), split(v_ref[0, 0])
    gcol = gcol_ref[0]
    gtot = gtot_ref[0]
    lane = lax.broadcasted_iota(jnp.int32, gcol.shape, 1)
    pick = lambda a, j: split(jnp.sum(jnp.where(lane == j, a, 0.0), axis=1, keepdims=True))
    js = [d * DN_HEADS + h for d in range(2)]
    gc_c = jnp.concatenate([pick(gcol, j) for j in js], axis=0)
    beta = jnp.concatenate([pick(gcol, 2 * DN_HEADS + j) for j in js], axis=0)
    tot = jnp.concatenate([pick(gtot, j) for j in js], axis=0)
    gc_r = jnp.concatenate([gcr_ref[0, :, pl.ds(j, 1), :] for j in js], axis=0)
    q2, k2, v2 = both(q3), both(k3), both(v3)
    incl = incl_ref[...]
    eye = (lax.broadcasted_iota(jnp.int32, (c, c), 0) == lax.broadcasted_iota(jnp.int32, (c, c), 1)).astype(F32)
    decay = jnp.where(incl > 0, jnp.exp(jnp.minimum(gc_c - gc_r, 0.0)), 0.0)
    kb = k2 * beta
    k2b = k2.astype(BF16)
    a = (incl - eye) * _bdot_nt(kb.astype(BF16), k2b) * decay
    t_inv = _tri_inverse(a, eye, bd_ref[...], off32_ref[...], off64_ref[...]).astype(BF16)
    e_gc = jnp.exp(gc_c)
    u = _bdot(t_inv, (v2 * beta).astype(BF16))
    w = _bdot(t_inv, (kb * e_gc).astype(BF16))
    qd = q2 * e_gc
    kd = k2 * jnp.exp(tot - gc_c)
    qk = both(_bdot_nt(q3.astype(BF16), k3.astype(BF16))) * decay
    for d in range(2):
        sl = slice(d * nch, (d + 1) * nch)
        u_ref[d, 0, 0] = u[sl].reshape(tb, HEAD_DIM)
        w_ref[d, 0, 0] = w[sl].reshape(tb, HEAD_DIM).astype(w_ref.dtype)
        qd_ref[d, 0, 0] = qd[sl].reshape(tb, HEAD_DIM).astype(qd_ref.dtype)
        kd_ref[d, 0, 0] = kd[sl].reshape(tb, HEAD_DIM).astype(kd_ref.dtype)
        qk_ref[d, 0, 0] = qk[sl].reshape(tb, c).astype(qk_ref.dtype)


def _dn_intra(q, k, v, gcol, gtot, gcr):
    b, nh, t, _ = q.shape
    tb = _pick(t, 256, CHUNK)
    nch = tb // CHUNK
    _, lv, incl, bd16 = _dir_consts()
    rep = lambda m: jnp.asarray(np.repeat(m, nch, axis=0))
    incl2, off32, off64 = rep(incl), rep(lv[:, 1]), rep(lv[:, 0])
    bd16 = jnp.asarray(bd16)
    hspec = pl.BlockSpec((1, 1, tb, HEAD_DIM), lambda bi, h, i: (bi, h, i, 0))
    tile = pl.BlockSpec((1, tb, LANES), lambda bi, h, i: (bi, i, 0))
    full = lambda a: pl.BlockSpec(a.shape, lambda bi, h, i: (0,) * a.ndim)
    ospec = pl.BlockSpec((2, 1, 1, tb, HEAD_DIM), lambda bi, h, i: (0, bi, h, i, 0))
    oshape = lambda dt: jax.ShapeDtypeStruct((2, b, nh, t, HEAD_DIM), dt)
    return pl.pallas_call(
        functools.partial(_dn_intra_kernel, nch=nch),
        grid=(b, nh, t // tb),
        in_specs=[
            hspec, hspec, hspec, tile, tile,
            pl.BlockSpec((1, nch, 2 * DN_HEADS, CHUNK), lambda bi, h, i: (bi, i, 0, 0)),
            full(incl2), full(off32), full(off64), full(bd16),
        ],
        out_specs=[ospec, ospec, ospec, ospec,
                   pl.BlockSpec((2, 1, 1, tb, CHUNK), lambda bi, h, i: (0, bi, h, i, 0))],
        out_shape=[oshape(F32), oshape(BF16), oshape(BF16), oshape(BF16),
                   jax.ShapeDtypeStruct((2, b, nh, t, CHUNK), BF16)],
        compiler_params=_cparams(("parallel", "parallel", "parallel")),
        name="dn_intra",
    )(q, k, v, gcol, gtot, gcr, incl2, off32, off64, bd16)


def _dn_scan_kernel(*refs):
    ins = refs[:12]
    of_ref, ob_ref, s_ref = refs[12:]

    @pl.when(pl.program_id(1) == 0)
    def _():
        s_ref[...] = jnp.zeros_like(s_ref)

    for d, o_ref in ((0, of_ref), (1, ob_ref)):
        u_ref, w_ref, qd_ref, kd_ref, qk_ref, gtot_ref = ins[6 * d:6 * d + 6]
        last = jnp.exp(gtot_ref[0, 0:1, :])
        for h in range(DN_HEADS):
            s = s_ref[d, h]
            wq = jnp.concatenate([w_ref[0, 0, h], qd_ref[0, 0, h]], axis=0)
            r = _dot(wq, s.astype(BF16))
            v_new = (u_ref[0, 0, h] - r[:CHUNK]).astype(BF16)
            o_ref[0, :, h * HEAD_DIM:(h + 1) * HEAD_DIM] = r[CHUNK:] + _dot(qk_ref[0, 0, h], v_new)
            j = d * DN_HEADS + h
            s_ref[d, h] = s * last[:, j:j + 1] + _dot_tn(kd_ref[0, 0, h], v_new)


def _dn_scan(u, w, qd, kd, qk, gtot, n_ctx):
    _, b, nh, t, _ = u.shape
    fwd, bwd, nc = _chunk_maps(n_ctx, t)

    def specs(d, order):
        hs = lambda width: pl.BlockSpec((1, 1, nh, CHUNK, width), lambda bi, i: (d, bi, 0, order(i), 0))
        return [hs(HEAD_DIM)] * 4 + [hs(CHUNK), pl.BlockSpec((1, CHUNK, LANES), lambda bi, i: (bi, order(i), 0))]

    args = (u, w, qd, kd, qk, gtot)
    return pl.pallas_call(
        _dn_scan_kernel,
        grid=(b, nc),
        in_specs=specs(0, fwd) + specs(1, bwd),
        out_specs=[
            pl.BlockSpec((1, CHUNK, DN_WIDTH), lambda bi, i: (bi, fwd(i), 0)),
            pl.BlockSpec((1, CHUNK, DN_WIDTH), lambda bi, i: (bi, bwd(i), 0)),
        ],
        out_shape=[jax.ShapeDtypeStruct((b, t, DN_WIDTH), F32)] * 2,
        scratch_shapes=[pltpu.VMEM((2, nh, HEAD_DIM, HEAD_DIM), F32)],
        compiler_params=_cparams(("parallel", "arbitrary")),
        name="dn_scan",
    )(*args, *args)


def _mix_finish_kernel(of_ref, ob_ref, z_ref, g_ref, o_ref):
    g = g_ref[...]
    for h in range(of_ref.shape[2] // HEAD_DIM):
        sl = slice(h * HEAD_DIM, (h + 1) * HEAD_DIM)
        o = of_ref[0, :, sl] + ob_ref[0, :, sl]
        y = o * lax.rsqrt(jnp.mean(o * o, axis=-1, keepdims=True) + EPS) * g
        o_ref[0, :, sl] = (y * _silu(z_ref[0, :, sl])).astype(o_ref.dtype)


def _mix_finish(o_f, o_b, p, gate_col, gain):
    b, t, width = o_f.shape
    tt = _pick(t, 256, SUBLANES)
    spec = pl.BlockSpec((1, tt, width), lambda bi, j: (bi, j, 0))
    return pl.pallas_call(
        _mix_finish_kernel,
        grid=(b, t // tt),
        in_specs=[spec, spec,
                  pl.BlockSpec((1, tt, width), lambda bi, j: (bi, j, gate_col // width)),
                  pl.BlockSpec((1, HEAD_DIM), lambda bi, j: (0, 0))],
        out_specs=spec,
        out_shape=jax.ShapeDtypeStruct((b, t, width), BF16),
        compiler_params=_cparams(("parallel", "parallel")),
        name="mix_finish",
    )(o_f, o_b, p, gain.astype(F32).reshape(1, HEAD_DIM))


def _out_proj_kernel(a1_ref, a2_ref, a3_ref, w1_ref, w2_ref, w3_ref, x_ref, g_ref, o_ref):
    y = _dot(a1_ref[0], w1_ref[...]) + _dot(a2_ref[0], w2_ref[...]) + _dot(a3_ref[0], w3_ref[...])
    o_ref[0] = x_ref[0] + g_ref[0] * y


def _out_proj(oa, ob, og, w_out, x, mod_rows, layer, n_ctx):
    b, t, d = x.shape
    tm = _pick(n_ctx, 256, SUBLANES)
    tn = _pick(d, 1024, LANES)
    nct = n_ctx // tm
    w = w_out.astype(BF16)
    return pl.pallas_call(
        _out_proj_kernel,
        grid=(d // tn, b, t // tm),
        in_specs=[
            pl.BlockSpec((1, tm, DN_WIDTH), lambda n, bi, i: (bi, i, 0)),
            pl.BlockSpec((1, tm, HG_WIDTH), lambda n, bi, i: (bi, i, 0)),
            pl.BlockSpec((1, tm, ATT_Q_WIDTH), lambda n, bi, i: (bi, i, 0)),
            pl.BlockSpec((DN_WIDTH, tn), lambda n, bi, i: (0, n)),
            pl.BlockSpec((HG_WIDTH, tn), lambda n, bi, i: (1, n)),
            pl.BlockSpec((ATT_Q_WIDTH, tn), lambda n, bi, i: (1, n)),
            pl.BlockSpec((1, tm, tn), lambda n, bi, i: (bi, i, n)),
            pl.BlockSpec((1, 1, tn), lambda n, bi, i: (
                (layer * SUBLANES + jnp.where(i < nct, b, bi)) * N_MOD + 2, 0, n)),
        ],
        out_specs=pl.BlockSpec((1, tm, tn), lambda n, bi, i: (bi, i, n)),
        out_shape=jax.ShapeDtypeStruct((b, t, d), F32),
        compiler_params=_cparams(("parallel", "parallel", "parallel")),
        name="out_proj",
    )(oa, ob, og, w, w, w, x, mod_rows)


def _router_kernel(h_ref, w_ref, b_ref, idx_ref, wt_ref, rank_ref, cnt_ref, carry_ref):
    i = pl.program_id(0)

    @pl.when(i == 0)
    def _():
        carry_ref[...] = jnp.zeros_like(carry_ref)

    tm = h_ref.shape[0]
    neg = -jnp.inf
    logits = _dot(h_ref[...].astype(BF16), w_ref[...])
    scores = _sigmoid(logits)
    lane = lax.broadcasted_iota(jnp.int32, scores.shape, 1)
    valid = lane < N_EXPERTS
    grp = lane // GROUP_SIZE
    big = jnp.int32(1 << 20)
    sel = jnp.where(valid, scores + b_ref[...], neg)
    rmax = lambda a: jnp.max(a, axis=1, keepdims=True)
    rmin = lambda a: jnp.min(a, axis=1, keepdims=True)

    grp_score = jnp.full(scores.shape, neg, F32)
    for g in range(N_GROUPS):
        in_g = grp == g
        cur = jnp.where(in_g, sel, neg)
        m1 = rmax(cur)
        i1 = rmin(jnp.where(cur == m1, lane, big))
        m2 = rmax(jnp.where(lane == i1, neg, cur))
        grp_score = jnp.where(in_g, m1 + m2, grp_score)
    keep = jnp.zeros(scores.shape, jnp.bool_)
    cur = grp_score
    for _ in range(TOPK_GROUPS):
        m = rmax(cur)
        gsel = rmin(jnp.where(cur == m, grp, big))
        hit = grp == gsel
        keep = jnp.logical_or(keep, hit)
        cur = jnp.where(hit, neg, cur)
    cur = jnp.where(keep, sel, neg)
    idx_out = jnp.zeros(scores.shape, jnp.int32)
    w_out = jnp.zeros(scores.shape, F32)
    onehots = []
    for k in range(TOP_K):
        m = rmax(cur)
        ik = rmin(jnp.where(cur == m, lane, big))
        hit = lane == ik
        sk = jnp.sum(jnp.where(hit, scores, 0.0), axis=1, keepdims=True)
        idx_out = jnp.where(lane == k, ik, idx_out)
        w_out = jnp.where(lane == k, sk, w_out)
        onehots.append(hit)
        cur = jnp.where(hit, neg, cur)
    w_out = w_out * (ROUTED_SCALE / jnp.sum(w_out, axis=1, keepdims=True))
    chosen = onehots[0]
    for oh in onehots[1:]:
        chosen = jnp.logical_or(chosen, oh)
    e_mat = jnp.where(chosen, 1.0, 0.0)
    r_i = lax.broadcasted_iota(jnp.int32, (tm, tm), 0)
    c_i = lax.broadcasted_iota(jnp.int32, (tm, tm), 1)
    lower = jnp.where(c_i < r_i, 1.0, 0.0).astype(BF16)
    before = _dot(lower, e_mat.astype(BF16)) + carry_ref[...]
    rank_out = jnp.zeros(scores.shape, jnp.int32)
    for k in range(TOP_K):
        rk = jnp.sum(jnp.where(onehots[k], before, 0.0), axis=1, keepdims=True)
        rank_out = jnp.where(lane == k, rk.astype(jnp.int32), rank_out)
    carry_ref[...] = carry_ref[...] + jnp.sum(e_mat, axis=0, keepdims=True)
    idx_ref[...] = idx_out
    wt_ref[...] = w_out
    rank_ref[...] = rank_out
    cnt_ref[...] = jnp.broadcast_to(carry_ref[...], cnt_ref.shape).astype(jnp.int32)


def _router(h, w_router, bias):
    n, d = h.shape
    tm = _pick(n, 256, SUBLANES)
    w = jnp.pad(w_router, ((0, 0), (0, LANES - N_EXPERTS))).astype(BF16)
    bz = jnp.pad(bias.astype(F32).reshape(1, -1), ((0, 0), (0, LANES - N_EXPERTS)))
    tile = pl.BlockSpec((tm, LANES), lambda i: (i, 0))
    return pl.pallas_call(
        _router_kernel,
        grid=(n // tm,),
        in_specs=[
            pl.BlockSpec((tm, d), lambda i: (i, 0)),
            pl.BlockSpec((d, LANES), lambda i: (0, 0)),
            pl.BlockSpec((1, LANES), lambda i: (0, 0)),
        ],
        out_specs=[tile, tile, tile, pl.BlockSpec((SUBLANES, LANES), lambda i: (0, 0))],
        out_shape=[
            jax.ShapeDtypeStruct((n, LANES), jnp.int32),
            jax.ShapeDtypeStruct((n, LANES), F32),
            jax.ShapeDtypeStruct((n, LANES), jnp.int32),
            jax.ShapeDtypeStruct((SUBLANES, LANES), jnp.int32),
        ],
        scratch_shapes=[pltpu.VMEM((1, LANES), F32)],
        compiler_params=_cparams(("arbitrary",)),
        name="router",
    )(h, w, bz)


def _pack_halves(y):
    half = y.shape[1] // 2
    bits = lambda v: lax.bitcast_convert_type(v.astype(BF16).astype(F32), jnp.uint32)
    return (bits(y[:, half:]) & jnp.uint32(0xFFFF0000)) | (bits(y[:, :half]) >> 16)


def _unpack_halves(w):
    lo = lax.bitcast_convert_type(w << 16, F32)
    hi = lax.bitcast_convert_type(w & jnp.uint32(0xFFFF0000), F32)
    return lo, hi


def _expert_kernel(be_ref, nu_ref, tok_ref, nx1_ref, nx2_ref, h_hbm, wg_ref, wu_ref, wd_ref, y_ref,
                   xbuf, wgb, wub, wdb, sem, *, n_steps):
    i = pl.program_id(0)
    n_used = nu_ref[0]
    slot = i % EXPERT_BUFS
    ahead = (i + 2) % EXPERT_BUFS

    def row_copy(tok, slot_, r):
        return pltpu.make_async_copy(h_hbm.at[pl.ds(tok, 1)], xbuf.at[slot_, pl.ds(r, 1)], sem.at[slot_])

    def start_gather(idx_ref, slot_):
        for r in range(MOE_ROWS):
            row_copy(idx_ref[0, 0, r], slot_, r).start()

    def wait_gather(slot_):
        pltpu.make_async_copy(h_hbm.at[pl.ds(0, MOE_ROWS)], xbuf.at[slot_], sem.at[slot_]).wait()

    @pl.when(i == 0)
    def _():
        start_gather(tok_ref, 0)
        start_gather(nx1_ref, 1)

    @pl.when(i <= n_used + 1)
    def _():
        wait_gather(slot)

    @pl.when(i < n_used)
    def _():
        @pl.when(jnp.logical_or(i == 0, be_ref[i] != be_ref[jnp.maximum(i - 1, 0)]))
        def _():
            wgb[...] = wg_ref[0, 0].astype(BF16)
            wub[...] = wu_ref[0, 0].astype(BF16)
            wdb[...] = wd_ref[0, 0].astype(BF16)

        x = xbuf[slot].astype(BF16)
        start_gather(nx2_ref, ahead)
        a = _silu(_dot(x, wgb[...])) * _dot(x, wub[...])
        y_ref[...] = _pack_halves(_dot(a.astype(BF16), wdb[...]))

    @pl.when(i >= n_used)
    def _():
        y_ref[...] = jnp.zeros_like(y_ref)

    @pl.when(i == n_steps - 1)
    def _():
        for j in (n_steps, n_steps + 1):
            @pl.when(j - 2 < n_used)
            def _():
                wait_gather(j % EXPERT_BUFS)


def _routed_experts(h, row_tok, block_e, n_used, wg, wu, wd, layer):
    n, d = h.shape
    n_blocks = block_e.shape[0]
    ff = wg.shape[3]
    assert n_blocks >= 2
    tok3 = row_tok.reshape(n_blocks, 1, MOE_ROWS)
    idx_spec = lambda k: pl.BlockSpec((1, 1, MOE_ROWS), lambda i, be, nu: (jnp.minimum(i + k, n_blocks - 1), 0, 0),
                                      memory_space=pltpu.SMEM)
    w_spec = lambda shape: pl.BlockSpec((1, 1) + shape, lambda i, be, nu: (layer, be[i], 0, 0),
                                        pipeline_mode=pl.Buffered(1))
    grid_spec = pltpu.PrefetchScalarGridSpec(
        num_scalar_prefetch=2,
        grid=(n_blocks,),
        in_specs=[
            idx_spec(0), idx_spec(1), idx_spec(2),
            pl.BlockSpec(memory_space=pl.ANY),
            w_spec((d, ff)), w_spec((d, ff)), w_spec((ff, d)),
        ],
        out_specs=pl.BlockSpec((MOE_ROWS, d // 2), lambda i, be, nu: (i, 0)),
        scratch_shapes=[pltpu.VMEM((EXPERT_BUFS, MOE_ROWS, d), F32),
                        pltpu.VMEM((d, ff), BF16), pltpu.VMEM((d, ff), BF16), pltpu.VMEM((ff, d), BF16),
                        pltpu.SemaphoreType.DMA((EXPERT_BUFS,))],
    )
    return pl.pallas_call(
        functools.partial(_expert_kernel, n_steps=n_blocks),
        grid_spec=grid_spec,
        out_shape=jax.ShapeDtypeStruct((n_blocks * MOE_ROWS, d // 2), jnp.uint32),
        compiler_params=_cparams(("arbitrary",)),
        name="routed_experts",
    )(block_e, n_used, tok3, tok3, tok3, h, wg, wu, wd)


def _shared_expert_kernel(h_ref, wsg_ref, wsu_ref, wsd_ref, o_ref):
    hb = h_ref[...].astype(BF16)
    a = _silu(_dot(hb, wsg_ref[...])) * _dot(hb, wsu_ref[...])
    o_ref[...] = _dot(a.astype(BF16), wsd_ref[...])


def _shared_expert(h, wsg, wsu, wsd):
    n, d = h.shape
    ff = wsg.shape[1]
    tm = _pick(n, 512, SUBLANES)
    tile = pl.BlockSpec((tm, d), lambda i: (i, 0))
    return pl.pallas_call(
        _shared_expert_kernel,
        grid=(n // tm,),
        in_specs=[tile,
                  pl.BlockSpec((d, ff), lambda i: (0, 0)),
                  pl.BlockSpec((d, ff), lambda i: (0, 0)),
                  pl.BlockSpec((ff, d), lambda i: (0, 0))],
        out_specs=tile,
        out_shape=jax.ShapeDtypeStruct((n, d), F32),
        compiler_params=_cparams(("parallel",)),
        name="shared_expert",
    )(h, wsg, wsu, wsd)


def _combine_kernel(pos_ref, nxt_ref, y_hbm, wt_ref, sh_ref, x_ref, g_ref, o_ref, ybuf, sem, *, tt):
    i = pl.program_id(0)
    n = pl.num_programs(0)
    slot = i % 2
    rows = tt * TOP_K

    def row_copy(src, slot_, r):
        return pltpu.make_async_copy(y_hbm.at[pl.ds(src, 1)], ybuf.at[slot_, pl.ds(r, 1)], sem.at[slot_])

    def start_gather(idx_ref, slot_):
        for r in range(rows):
            row_copy(idx_ref[0, 0, r], slot_, r).start()

    def wait_gather(slot_):
        pltpu.make_async_copy(y_hbm.at[pl.ds(0, rows)], ybuf.at[slot_], sem.at[slot_]).wait()

    @pl.when(i == 0)
    def _():
        start_gather(pos_ref, 0)

    wait_gather(slot)
    start_gather(nxt_ref, 1 - slot)
    wt = wt_ref[...]
    half = x_ref.shape[1] // 2
    cw = min(COMBINE_COLS, half)
    for c0 in range(0, half, cw):
        cl = slice(c0, c0 + cw)
        ch = slice(half + c0, half + c0 + cw)
        acc_lo = sh_ref[:, cl]
        acc_hi = sh_ref[:, ch]
        for k in range(TOP_K):
            lo, hi = _unpack_halves(ybuf[slot, k * tt:(k + 1) * tt, cl])
            acc_lo = acc_lo + wt[:, k:k + 1] * lo
            acc_hi = acc_hi + wt[:, k:k + 1] * hi
        o_ref[:, cl] = x_ref[:, cl] + g_ref[0, :, cl] * acc_lo
        o_ref[:, ch] = x_ref[:, ch] + g_ref[0, :, ch] * acc_hi

    @pl.when(i == n - 1)
    def _():
        wait_gather(1 - slot)


def _combine(y, pos, wts, shared, x, mod_rows, layer, batch, n_ctx):
    n, d = x.shape
    t = n // batch
    tt = _pick(n_ctx, 64, SUBLANES)
    n_tiles = n // tt
    tiles_per_b = t // tt
    nct = n_ctx // tt
    pos3 = pos.reshape(n_tiles, tt, TOP_K).transpose(0, 2, 1).reshape(n_tiles, 1, tt * TOP_K)
    row = lambda i: (layer * SUBLANES + jnp.where(i % tiles_per_b < nct, batch, i // tiles_per_b)) * N_MOD + 5
    kern = functools.partial(_combine_kernel, tt=tt)
    tile = pl.BlockSpec((tt, d), lambda i: (i, 0))
    return pl.pallas_call(
        kern,
        grid=(n_tiles,),
        in_specs=[
            pl.BlockSpec((1, 1, tt * TOP_K), lambda i: (i, 0, 0), memory_space=pltpu.SMEM),
            pl.BlockSpec((1, 1, tt * TOP_K), lambda i: (jnp.minimum(i + 1, n_tiles - 1), 0, 0),
                         memory_space=pltpu.SMEM),
            pl.BlockSpec(memory_space=pl.ANY),
            pl.BlockSpec((tt, LANES), lambda i: (i, 0)),
            tile, tile,
            pl.BlockSpec((1, 1, d), lambda i: (row(i), 0, 0)),
        ],
        out_specs=tile,
        out_shape=jax.ShapeDtypeStruct((n, d), F32),
        scratch_shapes=[pltpu.VMEM((2, tt * TOP_K, d // 2), jnp.uint32), pltpu.SemaphoreType.DMA((2,))],
        compiler_params=_cparams(("arbitrary",)),
        name="moe_combine",
    )(pos3, pos3, y, wts, shared, x, mod_rows)


def _moe(h, x, mod_rows, layer, w_router, router_bias, wg, wu, wd, wsg, wsu, wsd, batch, n_ctx):
    n, d = h.shape
    idx, wts, rank, counts = _router(h, w_router, router_bias)
    idx = idx[:, :TOP_K]
    rank = rank[:, :TOP_K]
    counts = counts[0, :N_EXPERTS]
    padded = (counts + MOE_ROWS - 1) // MOE_ROWS * MOE_ROWS
    pad_end = jnp.cumsum(padded)
    pad_start = pad_end - padded
    pos = pad_start[idx] + rank
    n_blocks = -(-(n * TOP_K) // MOE_ROWS) + N_EXPERTS
    tok = jnp.broadcast_to(jnp.arange(n, dtype=jnp.int32)[:, None], pos.shape)
    row_tok = jnp.zeros((n_blocks * MOE_ROWS,), jnp.int32).at[pos.reshape(-1)].set(tok.reshape(-1))
    block_start = jnp.arange(n_blocks, dtype=jnp.int32) * MOE_ROWS
    block_e = jnp.minimum(jnp.sum(pad_end[None, :] <= block_start[:, None], axis=1), N_EXPERTS - 1).astype(jnp.int32)
    n_used = (pad_end[-1] // MOE_ROWS).astype(jnp.int32).reshape(1)
    y = _routed_experts(h, row_tok, block_e, n_used, wg, wu, wd, layer)
    shared = _shared_expert(h, wsg, wsu, wsd)
    return _combine(y, pos.astype(jnp.int32), wts, shared, x, mod_rows, layer, batch, n_ctx)


def _rope_tables(n_ctx, n_lat):
    rows = n_lat // GRID_W
    row, col = jnp.meshgrid(jnp.arange(rows), jnp.arange(GRID_W), indexing="ij")
    half = HEAD_DIM // 2
    inv_freq = ROPE_THETA ** (-jnp.arange(0, half, 2, dtype=F32) / half)
    ang_r = row.reshape(-1, 1).astype(F32) * inv_freq
    ang_c = col.reshape(-1, 1).astype(F32) * inv_freq
    ang = jnp.concatenate([ang_r, ang_r, ang_c, ang_c], axis=-1)
    cos = jnp.concatenate([jnp.ones((n_ctx, HEAD_DIM), F32), jnp.cos(ang)], axis=0)
    sin = jnp.concatenate([jnp.zeros((n_ctx, HEAD_DIM), F32), jnp.sin(ang)], axis=0)
    first = (jnp.arange(HEAD_DIM) % 64) < 32
    return cos, jnp.where(first[None, :], -sin, sin)


def kernel(x, c, ctx, c_ctx, w_mod, b_mod, norm1_g, norm2_g, w_in, w_out, dn_conv_w, dn_a_log, dn_dt_bias, dn_norm_g, hg_lb_logits, hg_norm_g, att_q_norm_g, att_k_norm_g, w_router, router_bias, w_exp_gate, w_exp_up, w_exp_down, w_sh_gate, w_sh_up, w_sh_down, final_norm_g):
    bsz, n_lat, d = x.shape
    n_ctx = ctx.shape[1]
    depth = w_in.shape[0]
    t = n_ctx + n_lat
    assert bsz + 1 <= SUBLANES and n_ctx % CHUNK == 0 and n_lat % CHUNK == 0 and n_lat % GRID_W == 0

    cos, sin_s = _rope_tables(n_ctx, n_lat)
    lb_p = jax.nn.softmax(hg_lb_logits.astype(F32), axis=0)
    lower_bounds = jnp.cumsum(lb_p, axis=0) - lb_p[0]

    c_all = jnp.concatenate([c, c_ctx[None, :], jnp.zeros((SUBLANES - bsz - 1, d), c.dtype)], axis=0)
    mod = _mod_vectors(c_all, w_mod, b_mod)
    mod_rows = mod.reshape(depth * SUBLANES * N_MOD, 1, d)

    xs = jnp.concatenate([ctx, x], axis=1)
    for l in range(depth):
        h1 = _norm_mod(xs, norm1_g, mod_rows, l, 0, n_ctx, BF16)
        p = _matmul(h1.reshape(bsz * t, d), _permute_w_in(w_in[l]), F32).reshape(bsz, t, P_WIDTH)
        qr, kr, vr = _att_prep(p, cos, sin_s, att_q_norm_g[l], att_k_norm_g[l])
        og = _attention(qr, kr, vr, n_ctx)
        hf, hb = _hgrn2(p, lower_bounds[l], n_ctx)
        ob = _mix_finish(hf, hb, p, C_HG, hg_norm_g[l])
        dq, dk, dv, gcol, gtot, gcr = _dn_prep(p, dn_conv_w[l], dn_a_log[l], dn_dt_bias[l], n_ctx)
        u, w, qd, kd, qk = _dn_intra(dq, dk, dv, gcol, gtot, gcr)
        df, db = _dn_scan(u, w, qd, kd, qk, gtot, n_ctx)
        oa = _mix_finish(df, db, p, C_DZ, dn_norm_g[l])
        xs = _out_proj(oa, ob, og, w_out[l], xs, mod_rows, l, n_ctx)
        h2 = _norm_mod(xs, norm2_g, mod_rows, l, 3, n_ctx, F32).reshape(bsz * t, d)
        xs = _moe(h2, xs.reshape(bsz * t, d), mod_rows, l, w_router[l], router_bias[l],
                  w_exp_gate, w_exp_up, w_exp_down,
                  w_sh_gate[l].astype(BF16), w_sh_up[l].astype(BF16), w_sh_down[l].astype(BF16),
                  bsz, n_ctx).reshape(bsz, t, d)
    return _final_norm(xs, final_norm_g, n_ctx)
```

```python
import functools

import numpy as np
import jax
import jax.numpy as jnp
from jax import lax
from jax.experimental import pallas as pl
from jax.experimental.pallas import tpu as pltpu

F32 = jnp.float32
BF16 = jnp.bfloat16

EPS = 1e-6
GATE_FLOOR = 1e-30
HEAD_DIM = 128
DN_HEADS = 8
DN_WIDTH = DN_HEADS * HEAD_DIM
DN_CONV = 5
HG_HEADS = 8
HG_WIDTH = HG_HEADS * HEAD_DIM
ATT_Q_HEADS = 16
ATT_KV_HEADS = 4
ATT_GROUP = ATT_Q_HEADS // ATT_KV_HEADS
ATT_Q_WIDTH = ATT_Q_HEADS * HEAD_DIM
ATT_KV_WIDTH = ATT_KV_HEADS * HEAD_DIM
GRID_W = 64
ROPE_THETA = 10000.0
CHUNK = 64
N_EXPERTS = 64
TOP_K = 8
N_GROUPS = 8
TOPK_GROUPS = 4
GROUP_SIZE = N_EXPERTS // N_GROUPS
ROUTED_SCALE = 2.5
N_MOD = 6
LOG2E = 1.4426950408889634

C_AQ = 0
C_AK = C_AQ + ATT_Q_WIDTH
C_AV = C_AK + ATT_KV_WIDTH
C_DQ = C_AV + ATT_KV_WIDTH
C_DZ = C_DQ + 3 * DN_WIDTH
C_HQ = C_DZ + DN_WIDTH
C_HF = C_HQ + HG_WIDTH
C_HI = C_HF + 2 * HG_WIDTH
C_HG = C_HI + HG_WIDTH
C_AB = C_HG + HG_WIDTH
LANES = 128
SUBLANES = 8
P_WIDTH = C_AB + 4 * LANES

MOE_ROWS = 256
EXPERT_BUFS = 3
COMBINE_COLS = 512
VMEM_LIMIT = 56 * 1024 * 1024


def _cparams(sem, vmem=VMEM_LIMIT):
    return pltpu.CompilerParams(dimension_semantics=sem, vmem_limit_bytes=vmem)


def _pick(n, cap, mult):
    if n <= cap:
        return n
    for t in range(cap - cap % mult, 0, -mult):
        if n % t == 0:
            return t
    raise ValueError(f"no tile for {n} (cap {cap}, multiple of {mult})")


def _silu(x):
    return x * (1.0 / (1.0 + jnp.exp(-x)))


def _sigmoid(x):
    return 1.0 / (1.0 + jnp.exp(-x))


def _dot(a, b):
    return jnp.dot(a, b, preferred_element_type=F32)


def _dot_nt(a, b):
    return lax.dot_general(a, b, (((1,), (1,)), ((), ())), preferred_element_type=F32)


def _dot_tn(a, b):
    return lax.dot_general(a, b, (((0,), (0,)), ((), ())), preferred_element_type=F32)


def _split2(a):
    hi = a.astype(BF16)
    lo = (a - hi.astype(F32)).astype(BF16)
    return hi, lo


_LEVELS = (32, 16, 8, 4, 2, 1)


@functools.lru_cache(maxsize=None)
def _dir_consts():
    c = CHUNK
    t = np.arange(c)
    incl = [(t[None, :] <= t[:, None]), (t[None, :] >= t[:, None])]
    cms, masks = [], []
    for d in range(2):
        md = incl[d].astype(np.float32)
        blocks, mks = [md], []
        for b in _LEVELS:
            blk = t // (2 * b)
            ref = blk * 2 * b + (b - 1 if d == 0 else b)
            blocks.append(md[ref, :])
            same = blk[:, None] == blk[None, :]
            late = (t % (2 * b)) >= b
            if d == 0:
                mks.append(same & late[:, None] & ~late[None, :])
            else:
                mks.append(same & ~late[:, None] & late[None, :])
        mks.append(np.eye(c, dtype=bool))
        blocks.append(np.ones((c, c), np.float32))
        cms.append(np.concatenate(blocks, 0))
        masks.append(np.stack(mks).astype(np.float32))
    cm = np.stack(cms)
    lv = np.stack(masks)
    incl_f = np.stack([m.astype(np.float32) for m in incl])
    bd16 = ((t[:, None] // 16) == (t[None, :] // 16)).astype(np.float32)
    return cm, lv, incl_f, bd16


def _mod_kernel(c_ref, w_ref, b_ref, o_ref):
    a = _silu(c_ref[...]).astype(BF16)
    o_ref[0] = _dot(a, w_ref[0].astype(BF16)) + b_ref[0]


def _mod_vectors(c_all, w_mod, b_mod):
    depth, d, n = w_mod.shape
    tn = _pick(n, 512, LANES)
    return pl.pallas_call(
        _mod_kernel,
        grid=(depth, n // tn),
        in_specs=[
            pl.BlockSpec((SUBLANES, d), lambda l, j: (0, 0)),
            pl.BlockSpec((1, d, tn), lambda l, j: (l, 0, j)),
            pl.BlockSpec((1, 1, tn), lambda l, j: (l, 0, j)),
        ],
        out_specs=pl.BlockSpec((1, SUBLANES, tn), lambda l, j: (l, 0, j)),
        out_shape=jax.ShapeDtypeStruct((depth, SUBLANES, n), F32),
        compiler_params=_cparams(("parallel", "parallel")),
        name="mod_vectors",
    )(c_all, w_mod, b_mod.reshape(depth, 1, n))


def _norm_mod_kernel(x_ref, g_ref, sh_ref, sc_ref, o_ref):
    x = x_ref[0]
    var = jnp.mean(x * x, axis=-1, keepdims=True)
    y = x * lax.rsqrt(var + EPS) * g_ref[0]
    o_ref[0] = (y * (1.0 + sc_ref[0]) + sh_ref[0]).astype(o_ref.dtype)


def _norm_mod(x, gain, mod_rows, layer, k_shift, n_ctx, out_dtype):
    b, t, d = x.shape
    tt = _pick(n_ctx, 256, SUBLANES)
    nct = n_ctx // tt

    def mrow(k):
        return lambda bi, j: ((layer * SUBLANES + jnp.where(j < nct, b, bi)) * N_MOD + k, 0, 0)

    return pl.pallas_call(
        _norm_mod_kernel,
        grid=(b, t // tt),
        in_specs=[
            pl.BlockSpec((1, tt, d), lambda bi, j: (bi, j, 0)),
            pl.BlockSpec((1, 1, d), lambda bi, j: (layer, 0, 0)),
            pl.BlockSpec((1, 1, d), mrow(k_shift)),
            pl.BlockSpec((1, 1, d), mrow(k_shift + 1)),
        ],
        out_specs=pl.BlockSpec((1, tt, d), lambda bi, j: (bi, j, 0)),
        out_shape=jax.ShapeDtypeStruct((b, t, d), out_dtype),
        compiler_params=_cparams(("parallel", "parallel")),
        name="norm_mod",
    )(x, gain.reshape(gain.shape[0], 1, d), mod_rows, mod_rows)


def _final_norm_kernel(x_ref, g_ref, o_ref):
    x = x_ref[0]
    var = jnp.mean(x * x, axis=-1, keepdims=True)
    o_ref[0] = x * lax.rsqrt(var + EPS) * g_ref[...]


def _final_norm(x, gain, n_ctx):
    b, t, d = x.shape
    n_lat = t - n_ctx
    tt = _pick(n_ctx, 256, SUBLANES)
    off = n_ctx // tt
    return pl.pallas_call(
        _final_norm_kernel,
        grid=(b, n_lat // tt),
        in_specs=[
            pl.BlockSpec((1, tt, d), lambda bi, j: (bi, j + off, 0)),
            pl.BlockSpec((1, d), lambda bi, j: (0, 0)),
        ],
        out_specs=pl.BlockSpec((1, tt, d), lambda bi, j: (bi, j, 0)),
        out_shape=jax.ShapeDtypeStruct((b, n_lat, d), F32),
        compiler_params=_cparams(("parallel", "parallel")),
        name="final_norm",
    )(x, gain.reshape(1, d))


def _mm_kernel(a_ref, w_ref, o_ref):
    o_ref[...] = _dot(a_ref[...], w_ref[...]).astype(o_ref.dtype)


def _matmul(a, w, out_dtype):
    m, k = a.shape
    n = w.shape[1]
    tm = _pick(m, 1536, 256) if m > 1536 else m
    tn = _pick(n, 512, LANES)
    return pl.pallas_call(
        _mm_kernel,
        grid=(m // tm, n // tn),
        in_specs=[
            pl.BlockSpec((tm, k), lambda i, j: (i, 0)),
            pl.BlockSpec((k, tn), lambda i, j: (0, j)),
        ],
        out_specs=pl.BlockSpec((tm, tn), lambda i, j: (i, j)),
        out_shape=jax.ShapeDtypeStruct((m, n), out_dtype),
        compiler_params=_cparams(("parallel", "parallel")),
        name="in_proj",
    )(a, w)


def _permute_w_in(w):
    d = w.shape[0]
    dn_cols = 4 * DN_WIDTH + 4 * DN_HEADS
    hg_cols = 5 * HG_WIDTH
    att0 = dn_cols + hg_cols
    parts = [
        w[:, att0:],
        w[:, :4 * DN_WIDTH],
        w[:, dn_cols:att0],
        w[:, 4 * DN_WIDTH:dn_cols],
        jnp.zeros((d, P_WIDTH - C_AB - 4 * DN_HEADS), w.dtype),
    ]
    return jnp.concatenate(parts, axis=1).astype(BF16)


def _att_prep_kernel(q_ref, k_ref, v_ref, cos_ref, sin_ref, qg_ref, kg_ref, qo_ref, ko_ref, vo_ref):
    cos = cos_ref[...]
    sin = sin_ref[...]
    lane = lax.broadcasted_iota(jnp.int32, cos.shape, 1)
    first = (lane % 64) < 32

    def norm_rope(x, g, scale):
        y = x * lax.rsqrt(jnp.mean(x * x, axis=-1, keepdims=True) + EPS) * g
        r = jnp.where(first, pltpu.roll(y, HEAD_DIM - 32, 1), pltpu.roll(y, 32, 1))
        return (y * cos + r * sin) * scale

    q = q_ref[0]
    qg = qg_ref[...]
    for h in range(ATT_Q_HEADS):
        sl = slice(h * HEAD_DIM, (h + 1) * HEAD_DIM)
        qo_ref[0, :, sl] = norm_rope(q[:, sl], qg, HEAD_DIM ** -0.5 * LOG2E).astype(qo_ref.dtype)
    k = k_ref[0]
    kg = kg_ref[...]
    for h in range(ATT_KV_HEADS):
        sl = slice(h * HEAD_DIM, (h + 1) * HEAD_DIM)
        ko_ref[0, :, sl] = norm_rope(k[:, sl], kg, 1.0).astype(ko_ref.dtype)
    vo_ref[0] = v_ref[0].astype(vo_ref.dtype)


def _att_prep(p, cos, sin_s, qg, kg):
    b, t, _ = p.shape
    tt = _pick(t, 256, SUBLANES)
    kvb = C_AK // ATT_KV_WIDTH
    return pl.pallas_call(
        _att_prep_kernel,
        grid=(b, t // tt),
        in_specs=[
            pl.BlockSpec((1, tt, ATT_Q_WIDTH), lambda bi, j: (bi, j, 0)),
            pl.BlockSpec((1, tt, ATT_KV_WIDTH), lambda bi, j: (bi, j, kvb)),
            pl.BlockSpec((1, tt, ATT_KV_WIDTH), lambda bi, j: (bi, j, kvb + 1)),
            pl.BlockSpec((tt, HEAD_DIM), lambda bi, j: (j, 0)),
            pl.BlockSpec((tt, HEAD_DIM), lambda bi, j: (j, 0)),
            pl.BlockSpec((1, HEAD_DIM), lambda bi, j: (0, 0)),
            pl.BlockSpec((1, HEAD_DIM), lambda bi, j: (0, 0)),
        ],
        out_specs=[
            pl.BlockSpec((1, tt, ATT_Q_WIDTH), lambda bi, j: (bi, j, 0)),
            pl.BlockSpec((1, tt, ATT_KV_WIDTH), lambda bi, j: (bi, j, 0)),
            pl.BlockSpec((1, tt, ATT_KV_WIDTH), lambda bi, j: (bi, j, 0)),
        ],
        out_shape=[
            jax.ShapeDtypeStruct((b, t, ATT_Q_WIDTH), BF16),
            jax.ShapeDtypeStruct((b, t, ATT_KV_WIDTH), BF16),
            jax.ShapeDtypeStruct((b, t, ATT_KV_WIDTH), BF16),
        ],
        compiler_params=_cparams(("parallel", "parallel")),
        name="att_prep",
    )(p, p, p, cos, sin_s, qg.reshape(1, HEAD_DIM), kg.reshape(1, HEAD_DIM))


ATT_ROW_BLOCK = 512
ATT_KEY_TILE = 256
ATT_KEY_CHUNK = 2048


def _attn_kernel(q_ref, k_ref, v_ref, o_ref, qs_ref, sa_ref, sb_ref, pa_ref, pb_ref, ala_ref, alb_ref,
                 m_ref, l_ref, acc_ref, *, tq, tk, n_ctx, lat_chunks, ctx_qtiles):
    i = pl.program_id(2)
    rows = ATT_GROUP * tq
    for g in range(ATT_GROUP):
        qs_ref[g * tq:(g + 1) * tq, :] = q_ref[0, :, g * HEAD_DIM:(g + 1) * HEAD_DIM]

    n_blocks = rows // ATT_ROW_BLOCK

    def max_pass(s_ref, al_ref, width, first, blocks):
        for r in blocks:
            rs = slice(r * ATT_ROW_BLOCK, (r + 1) * ATT_ROW_BLOCK)
            mx = s_ref[rs, 0:LANES]
            for c0 in range(LANES, width, LANES):
                mx = jnp.maximum(mx, s_ref[rs, c0:c0 + LANES])
            m_new = jnp.max(mx, axis=1, keepdims=True)
            if first:
                al_ref[rs, :] = jnp.ones_like(m_new)
            else:
                m_old = m_ref[rs, :]
                m_new = jnp.maximum(m_old, m_new)
                al_ref[rs, :] = jnp.exp2(m_old - m_new)
            m_ref[rs, :] = m_new

    def exp_pass(s_ref, p_ref, al_ref, width, first, blocks):
        for r in blocks:
            rs = slice(r * ATT_ROW_BLOCK, (r + 1) * ATT_ROW_BLOCK)
            m_new = m_ref[rs, :]
            psum = jnp.zeros((ATT_ROW_BLOCK, LANES), F32)
            for c0 in range(0, width, LANES):
                pr = jnp.exp2(s_ref[rs, c0:c0 + LANES] - m_new)
                psum = psum + pr
                p_ref[rs, c0:c0 + LANES] = pr.astype(p_ref.dtype)
            lsum = jnp.sum(psum, axis=1, keepdims=True)
            if first:
                l_ref[rs, :] = lsum
            else:
                l_ref[rs, :] = al_ref[rs, :] * l_ref[rs, :] + lsum

    def softmax_update(s_ref, p_ref, al_ref, width, first):
        max_pass(s_ref, al_ref, width, first, range(n_blocks))
        exp_pass(s_ref, p_ref, al_ref, width, first, range(n_blocks))

    def write_out():
        out = acc_ref[...] * (1.0 / l_ref[...])
        for g in range(ATT_GROUP):
            o_ref[0, :, g * HEAD_DIM:(g + 1) * HEAD_DIM] = out[g * tq:(g + 1) * tq].astype(o_ref.dtype)

    sb_ref[:, :n_ctx] = _dot_nt(qs_ref[...], k_ref[0, 0:n_ctx, :])

    @pl.when(i < ctx_qtiles)
    def _():
        softmax_update(sb_ref, pb_ref, alb_ref, n_ctx, True)
        acc_ref[...] = _dot(pb_ref[:, :n_ctx], v_ref[0, 0:n_ctx, :])
        write_out()

    @pl.when(i >= ctx_qtiles)
    def _():
        def rows_at(ref, start):
            if isinstance(start, int):
                return ref[0, start:start + tk, :]
            return ref[0, pl.ds(pl.multiple_of(start, LANES), tk), :]

        def k_chunk(c):
            return rows_at(k_ref, n_ctx + c * tk)

        def v_before(c):
            start = n_ctx + (c - 1) * tk
            return rows_at(v_ref, max(start, 0) if isinstance(start, int) else jnp.maximum(start, 0))

        def pv(p_ref, al_ref, vc):
            acc_ref[...] = al_ref[...] * acc_ref[...] + _dot(p_ref[...], vc)

        def stage(c, s_cur, p_cur, al_cur, s_nxt, p_prev, al_prev):
            pv(p_prev, al_prev, v_before(c))
            max_pass(s_cur, al_cur, tk, False, range(n_blocks))
            kt = min(ATT_KEY_TILE, tk)
            parts = tk // kt
            k_next = k_chunk(c + 1) if s_nxt is not None else None
            for j in range(parts):
                if s_nxt is not None:
                    ks = slice(j * kt, (j + 1) * kt)
                    s_nxt[:, ks] = _dot_nt(qs_ref[...], k_next[ks])
                exp_pass(s_cur, p_cur, al_cur, tk, False,
                         range(j * n_blocks // parts, (j + 1) * n_blocks // parts))

        a_bufs = (sa_ref, pa_ref, ala_ref)
        b_bufs = (sb_ref, pb_ref, alb_ref)
        if n_ctx < tk:
            pb_ref[:, n_ctx:] = jnp.zeros((rows, tk - n_ctx), pb_ref.dtype)
        acc_ref[...] = jnp.zeros_like(acc_ref)
        sa_ref[...] = _dot_nt(qs_ref[...], k_chunk(0))
        softmax_update(sb_ref, pb_ref, alb_ref, n_ctx, True)

        for c0 in range(0, lat_chunks - 2, 2):
            stage(c0, *a_bufs, sb_ref, pb_ref, alb_ref)
            stage(c0 + 1, *b_bufs, sa_ref, pa_ref, ala_ref)
        stage(lat_chunks - 2, *a_bufs, sb_ref, pb_ref, alb_ref)
        stage(lat_chunks - 1, *b_bufs, None, pa_ref, ala_ref)
        pv(pb_ref, alb_ref, v_before(lat_chunks))
        write_out()


def _attention(qr, kr, vr, n_ctx):
    b, t, _ = qr.shape
    n_lat = t - n_ctx
    tq = _pick(n_ctx, 256, SUBLANES)
    tk = _pick(n_lat // 2, ATT_KEY_CHUNK, LANES)
    assert n_lat % (2 * tk) == 0 and n_ctx % LANES == 0 and n_ctx <= tk
    rows = ATT_GROUP * tq
    gw = ATT_GROUP * HEAD_DIM
    kern = functools.partial(_attn_kernel, tq=tq, tk=tk, n_ctx=n_ctx, lat_chunks=n_lat // tk,
                             ctx_qtiles=n_ctx // tq)
    return pl.pallas_call(
        kern,
        grid=(b, ATT_KV_HEADS, t // tq),
        in_specs=[
            pl.BlockSpec((1, tq, gw), lambda bi, h, i: (bi, i, h)),
            pl.BlockSpec((1, t, HEAD_DIM), lambda bi, h, i: (bi, 0, h)),
            pl.BlockSpec((1, t, HEAD_DIM), lambda bi, h, i: (bi, 0, h)),
        ],
        out_specs=pl.BlockSpec((1, tq, gw), lambda bi, h, i: (bi, i, h)),
        out_shape=jax.ShapeDtypeStruct((b, t, ATT_Q_WIDTH), BF16),
        scratch_shapes=[
            pltpu.VMEM((rows, HEAD_DIM), BF16),
            pltpu.VMEM((rows, tk), F32), pltpu.VMEM((rows, tk), F32),
            pltpu.VMEM((rows, tk), BF16), pltpu.VMEM((rows, tk), BF16),
            pltpu.VMEM((rows, 1), F32), pltpu.VMEM((rows, 1), F32),
            pltpu.VMEM((rows, 1), F32), pltpu.VMEM((rows, 1), F32),
            pltpu.VMEM((rows, HEAD_DIM), F32),
        ],
        compiler_params=_cparams(("parallel", "parallel", "parallel")),
        name="attention",
    )(qr, kr, vr)


def _chunk_maps(n_ctx, t, size=CHUNK):
    ncc = n_ctx // size
    nc = t // size

    def fwd(i):
        return i

    def bwd(i):
        return jnp.where(i < ncc, ncc - 1 - i, nc - 1 + ncc - i)

    return fwd, bwd, nc


def _bdot_tn(a, b):
    return lax.dot_general(a, b, (((1,), (1,)), ((0,), (0,))), preferred_element_type=F32)


def _hg_kernel(qf_ref, ff_ref, if_ref, qb_ref, fb_ref, ib_ref, lbf_ref, lbb_ref, cm_ref, lv_ref,
               of_ref, ob_ref, s_ref, *, nch):
    @pl.when(pl.program_id(2) == 0)
    def _():
        s_ref[...] = jnp.zeros_like(s_ref)

    c = CHUNK
    split = lambda a: a.reshape(nch, c, HEAD_DIM)

    def gates(q_raw, f_raw, lb):
        one_m_lb = 1.0 - lb
        f_gate = lb + one_m_lb * _sigmoid(f_raw)
        log_f = jnp.log(jnp.maximum(f_gate, GATE_FLOOR))
        return split(_silu(q_raw)), split(one_m_lb * _sigmoid(-f_raw)), split(log_f)

    parts = [gates(qf_ref[0], ff_ref[0], lbf_ref[0]), gates(qb_ref[0], fb_ref[0], lbb_ref[0])]
    q, key, log_f = (jnp.concatenate([parts[0][n], parts[1][n]], axis=0) for n in range(3))
    vb = jnp.concatenate([split(if_ref[0]), split(ib_ref[0])], axis=0).astype(BF16)
    hi = log_f.astype(BF16)
    r1 = log_f - hi.astype(F32)
    mid = r1.astype(BF16)
    lo = (r1 - mid.astype(F32)).astype(BF16)
    cm = cm_ref[...]
    x = _bdot(cm, hi) + _bdot(cm, mid) + _bdot(cm, lo)
    cum = x[:, 0:c]
    tot = x[:, 7 * c:7 * c + 1]
    lv = lv_ref[...]
    att = jnp.where(lv[:, 6] > 0, _bdot_nt(q.astype(BF16), key.astype(BF16)), 0.0)
    for n in range(len(_LEVELS)):
        ref = x[:, (n + 1) * c:(n + 2) * c]
        qs = (q * jnp.exp(jnp.minimum(cum - ref, 0.0))).astype(BF16)
        ks = (key * jnp.exp(jnp.minimum(ref - cum, 0.0))).astype(BF16)
        att = att + jnp.where(lv[:, n] > 0, _bdot_nt(qs, ks), 0.0)
    o_intra = _bdot(att.astype(BF16), vb)
    q_dec = (q * jnp.exp(cum)).astype(BF16)
    k_dec = (key * jnp.exp(tot - cum)).astype(BF16)
    kv = _bdot_tn(vb, k_dec)
    decay = jnp.exp(tot)
    for d, o_ref in ((0, of_ref), (1, ob_ref)):
        st = s_ref[d]
        for ci in (range(nch) if d == 0 else range(nch - 1, -1, -1)):
            n = d * nch + ci
            o_ref[0, ci * c:(ci + 1) * c, :] = o_intra[n] + _dot_nt(q_dec[n], st.astype(BF16))
            st = st * decay[n] + kv[n]
        s_ref[d] = st


def _hgrn2(p, lower_bound, n_ctx):
    b, t, _ = p.shape
    nch = min(4, n_ctx // CHUNK)
    tb = nch * CHUNK
    fwd, bwd, nblk = _chunk_maps(n_ctx, t, tb)
    cm, lv, _, _ = _dir_consts()
    cm = jnp.asarray(np.repeat(cm, nch, axis=0), BF16)
    lv = jnp.asarray(np.repeat(lv, nch, axis=0), F32)
    lb = lower_bound.astype(F32).reshape(2 * HG_HEADS, 1, HEAD_DIM)
    hq, hf, hi_ = C_HQ // HEAD_DIM, C_HF // HEAD_DIM, C_HI // HEAD_DIM
    blk = (1, tb, HEAD_DIM)

    def col(base, order):
        return pl.BlockSpec(blk, lambda bi, h, i: (bi, order(i), base + h))

    full = lambda shape: pl.BlockSpec(shape, lambda bi, h, i: (0,) * len(shape))
    return pl.pallas_call(
        functools.partial(_hg_kernel, nch=nch),
        grid=(b, HG_HEADS, nblk),
        in_specs=[
            col(hq, fwd), col(hf, fwd), col(hi_, fwd),
            col(hq, bwd), col(hf + HG_HEADS, bwd), col(hi_, bwd),
            pl.BlockSpec((1, 1, HEAD_DIM), lambda bi, h, i: (h, 0, 0)),
            pl.BlockSpec((1, 1, HEAD_DIM), lambda bi, h, i: (HG_HEADS + h, 0, 0)),
            full(cm.shape), full(lv.shape),
        ],
        out_specs=[
            pl.BlockSpec(blk, lambda bi, h, i: (bi, fwd(i), h)),
            pl.BlockSpec(blk, lambda bi, h, i: (bi, bwd(i), h)),
        ],
        out_shape=[jax.ShapeDtypeStruct((b, t, HG_WIDTH), F32)] * 2,
        scratch_shapes=[pltpu.VMEM((2, HEAD_DIM, HEAD_DIM), F32)],
        compiler_params=_cparams(("parallel", "parallel", "arbitrary")),
        name="hgrn2",
    )(p, p, p, p, p, p, lb, lb, cm, lv)


def _dn_prep_kernel(x_ref, prev_ref, next_ref, ab_ref, w_ref, alog_ref, dt_ref, cs_ref,
                    q_ref, k_ref, v_ref, gcol_ref, gtot_ref, gct_ref, *, tt, ctx_tiles, n_tiles):
    j = pl.program_id(1)
    seg_start = jnp.logical_or(j == 0, j == ctx_tiles)
    seg_end = jnp.logical_or(j == ctx_tiles - 1, j == n_tiles - 1)
    x = x_ref[0]
    ph = jnp.where(seg_start, 0.0, prev_ref[0])
    nh = jnp.where(seg_end, 0.0, next_ref[0])
    w = w_ref[0]
    r8 = lax.broadcasted_iota(jnp.int32, (SUBLANES, x.shape[1]), 0)
    half = DN_CONV // 2

    def later(s):
        r = pltpu.roll(x, tt - s, 0)
        tail = jnp.where(r8 >= SUBLANES - s, pltpu.roll(nh, SUBLANES - s, 0), r[tt - SUBLANES:])
        return jnp.concatenate([r[:tt - SUBLANES], tail], axis=0)

    def earlier(s):
        r = pltpu.roll(x, s, 0)
        head = jnp.where(r8 < s, pltpu.roll(ph, s, 0), r[:SUBLANES])
        return jnp.concatenate([head, r[SUBLANES:]], axis=0)

    acc = x * w[half:half + 1]
    for s in range(1, half + 1):
        acc = acc + later(s) * w[half + s:half + s + 1] + earlier(s) * w[half - s:half - s + 1]
    y = _silu(acc)
    for h in range(DN_HEADS):
        qh = y[:, h * HEAD_DIM:(h + 1) * HEAD_DIM]
        kh = y[:, DN_WIDTH + h * HEAD_DIM:DN_WIDTH + (h + 1) * HEAD_DIM]
        q_ref[0, h] = qh * lax.rsqrt(jnp.sum(qh * qh, axis=-1, keepdims=True) + EPS) * HEAD_DIM ** -0.5
        k_ref[0, h] = kh * lax.rsqrt(jnp.sum(kh * kh, axis=-1, keepdims=True) + EPS)
        v_ref[0, h] = y[:, 2 * DN_WIDTH + h * HEAD_DIM:2 * DN_WIDTH + (h + 1) * HEAD_DIM]
    a = ab_ref[0]
    z = a + dt_ref[...]
    softplus = jnp.maximum(z, 0.0) + jnp.log1p(jnp.exp(-jnp.abs(z)))
    lane = lax.broadcasted_iota(jnp.int32, a.shape, 1)
    gb = jnp.where(lane < 2 * DN_HEADS, -jnp.exp(alog_ref[...]) * softplus, _sigmoid(a))
    hi = gb.astype(BF16)
    r1 = gb - hi.astype(F32)
    mid = r1.astype(BF16)
    lo = (r1 - mid.astype(F32)).astype(BF16)
    cmat = cs_ref[...]
    xs = _dot(cmat, hi) + _dot(cmat, mid) + _dot(cmat, lo)
    gcol = jnp.where(lane < DN_HEADS, xs[:tt], jnp.where(lane < 2 * DN_HEADS, xs[tt:2 * tt], gb))
    gcol_ref[0] = gcol
    gtot_ref[0] = xs[2 * tt:]
    gct_ref[0] = gcol.T


def _dn_prep(p, conv_w, a_log, dt_bias, n_ctx):
    b, t, _ = p.shape
    tt = _pick(n_ctx, 128, LANES)
    n_tiles = t // tt
    r = tt // SUBLANES
    last8 = t // SUBLANES - 1
    cw = 3 * DN_WIDTH
    cb = C_DQ // cw
    pad = lambda v: jnp.pad(v.astype(F32).reshape(1, -1), ((0, 0), (0, LANES - 2 * DN_HEADS)))
    kern = functools.partial(_dn_prep_kernel, tt=tt, ctx_tiles=n_ctx // tt, n_tiles=n_tiles)
    hshape = jax.ShapeDtypeStruct((b, DN_HEADS, t, HEAD_DIM), F32)
    hspec = pl.BlockSpec((1, DN_HEADS, tt, HEAD_DIM), lambda bi, j: (bi, 0, j, 0))
    _, _, incl, _ = _dir_consts()
    eye = np.eye(tt // CHUNK, dtype=np.float32)
    cs = np.concatenate([np.kron(eye, incl[0]), np.kron(eye, incl[1]),
                         np.kron(eye, np.ones((CHUNK, CHUNK), np.float32))], axis=0)
    tile = pl.BlockSpec((1, tt, LANES), lambda bi, j: (bi, j, 0))
    q, k, v, gcol, gtot, gct = pl.pallas_call(
        kern,
        grid=(b, n_tiles),
        in_specs=[
            pl.BlockSpec((1, tt, cw), lambda bi, j: (bi, j, cb)),
            pl.BlockSpec((1, SUBLANES, cw), lambda bi, j: (bi, jnp.maximum(j * r - 1, 0), cb)),
            pl.BlockSpec((1, SUBLANES, cw), lambda bi, j: (bi, jnp.minimum((j + 1) * r, last8), cb)),
            pl.BlockSpec((1, tt, LANES), lambda bi, j: (bi, j, C_AB // LANES)),
            pl.BlockSpec((1, DN_CONV, cw), lambda bi, j: (0, 0, 0)),
            pl.BlockSpec((1, LANES), lambda bi, j: (0, 0)),
            pl.BlockSpec((1, LANES), lambda bi, j: (0, 0)),
            pl.BlockSpec((3 * tt, tt), lambda bi, j: (0, 0)),
        ],
        out_specs=[hspec, hspec, hspec, tile, tile,
                   pl.BlockSpec((1, LANES, tt), lambda bi, j: (bi, 0, j))],
        out_shape=[hshape, hshape, hshape,
                   jax.ShapeDtypeStruct((b, t, LANES), F32),
                   jax.ShapeDtypeStruct((b, t, LANES), F32),
                   jax.ShapeDtypeStruct((b, LANES, t), F32)],
        compiler_params=_cparams(("parallel", "parallel")),
        name="dn_prep",
    )(p, p, p, p, conv_w.astype(F32)[None], pad(a_log), pad(dt_bias), jnp.asarray(cs, BF16))
    gcr = gct[:, :2 * DN_HEADS].reshape(b, 2 * DN_HEADS, t // CHUNK, CHUNK).transpose(0, 2, 1, 3)
    return q, k, v, gcol, gtot, gcr


def _bdot(a, b):
    return lax.dot_general(a, b, (((2,), (1,)), ((0,), (0,))), preferred_element_type=F32)


def _bdot_nt(a, b):
    return lax.dot_general(a, b, (((2,), (2,)), ((0,), (0,))), preferred_element_type=F32)


def _bdot3(a, b):
    ah, al = _split2(a)
    bh, bl = _split2(b)
    return _bdot(ah, bh) + _bdot(ah, bl) + _bdot(al, bh)


def _tri_inverse(a, eye, bd16, off32, off64):
    n = -(a * bd16)
    s = eye + n
    p = n
    for _ in range(3):
        p = _bdot3(p, p)
        s = s + _bdot3(p, s)
    for off in (off32, off64):
        s = s - _bdot3(_bdot3(s, a * off), s)
    return s


def _dn_intra_kernel(q_ref, k_ref, v_ref, gcol_ref, gtot_ref, gcr_ref, incl_ref, off32_ref, off64_ref, bd_ref,
                     u_ref, w_ref, qd_ref, kd_ref, qk_ref, *, nch):
    h = pl.program_id(1)
    c = CHUNK
    tb = nch * c
    split = lambda a: a.reshape(nch, c, a.shape[-1])
    both = lambda a: jnp.concatenate([a, a], axis=0)
    q3, k3, v3 = split(q_ref[0, 0]), split(k_ref[0, 0]), split(v_ref[0, 0])
    gcol = gcol_ref[0]
    gtot = gtot_ref[0]
    lane = lax.broadcasted_iota(jnp.int32, gcol.shape, 1)
    pick = lambda a, j: split(jnp.sum(jnp.where(lane == j, a, 0.0), axis=1, keepdims=True))
    js = [d * DN_HEADS + h for d in range(2)]
    gc_c = jnp.concatenate([pick(gcol, j) for j in js], axis=0)
    beta = jnp.concatenate([pick(gcol, 2 * DN_HEADS + j) for j in js], axis=0)
    tot = jnp.concatenate([pick(gtot, j) for j in js], axis=0)
    gc_r = jnp.concatenate([gcr_ref[0, :, pl.ds(j, 1), :] for j in js], axis=0)
    q2, k2, v2 = both(q3), both(k3), both(v3)
    incl = incl_ref[...]
    eye = (lax.broadcasted_iota(jnp.int32, (c, c), 0) == lax.broadcasted_iota(jnp.int32, (c, c), 1)).astype(F32)
    decay = jnp.where(incl > 0, jnp.exp(jnp.minimum(gc_c - gc_r, 0.0)), 0.0)
    kb = k2 * beta
    k2b = k2.astype(BF16)
    a = (incl - eye) * _bdot_nt(kb.astype(BF16), k2b) * decay
    t_inv = _tri_inverse(a, eye, bd_ref[...], off32_ref[...], off64_ref[...]).astype(BF16)
    e_gc = jnp.exp(gc_c)
    u = _bdot(t_inv, (v2 * beta).astype(BF16))
    w = _bdot(t_inv, (kb * e_gc).astype(BF16))
    qd = q2 * e_gc
    kd = k2 * jnp.exp(tot - gc_c)
    qk = both(_bdot_nt(q3.astype(BF16), k3.astype(BF16))) * decay
    for d in range(2):
        sl = slice(d * nch, (d + 1) * nch)
        u_ref[d, 0, 0] = u[sl].reshape(tb, HEAD_DIM)
        w_ref[d, 0, 0] = w[sl].reshape(tb, HEAD_DIM).astype(w_ref.dtype)
        qd_ref[d, 0, 0] = qd[sl].reshape(tb, HEAD_DIM).astype(qd_ref.dtype)
        kd_ref[d, 0, 0] = kd[sl].reshape(tb, HEAD_DIM).astype(kd_ref.dtype)
        qk_ref[d, 0, 0] = qk[sl].reshape(tb, c).astype(qk_ref.dtype)


def _dn_intra(q, k, v, gcol, gtot, gcr):
    b, nh, t, _ = q.shape
    tb = _pick(t, 256, CHUNK)
    nch = tb // CHUNK
    _, lv, incl, bd16 = _dir_consts()
    rep = lambda m: jnp.asarray(np.repeat(m, nch, axis=0))
    incl2, off32, off64 = rep(incl), rep(lv[:, 1]), rep(lv[:, 0])
    bd16 = jnp.asarray(bd16)
    hspec = pl.BlockSpec((1, 1, tb, HEAD_DIM), lambda bi, h, i: (bi, h, i, 0))
    tile = pl.BlockSpec((1, tb, LANES), lambda bi, h, i: (bi, i, 0))
    full = lambda a: pl.BlockSpec(a.shape, lambda bi, h, i: (0,) * a.ndim)
    ospec = pl.BlockSpec((2, 1, 1, tb, HEAD_DIM), lambda bi, h, i: (0, bi, h, i, 0))
    oshape = lambda dt: jax.ShapeDtypeStruct((2, b, nh, t, HEAD_DIM), dt)
    return pl.pallas_call(
        functools.partial(_dn_intra_kernel, nch=nch),
        grid=(b, nh, t // tb),
        in_specs=[
            hspec, hspec, hspec, tile, tile,
            pl.BlockSpec((1, nch, 2 * DN_HEADS, CHUNK), lambda bi, h, i: (bi, i, 0, 0)),
            full(incl2), full(off32), full(off64), full(bd16),
        ],
        out_specs=[ospec, ospec, ospec, ospec,
                   pl.BlockSpec((2, 1, 1, tb, CHUNK), lambda bi, h, i: (0, bi, h, i, 0))],
        out_shape=[oshape(F32), oshape(BF16), oshape(BF16), oshape(BF16),
                   jax.ShapeDtypeStruct((2, b, nh, t, CHUNK), BF16)],
        compiler_params=_cparams(("parallel", "parallel", "parallel")),
        name="dn_intra",
    )(q, k, v, gcol, gtot, gcr, incl2, off32, off64, bd16)


def _dn_scan_kernel(*refs):
    ins = refs[:12]
    of_ref, ob_ref, s_ref = refs[12:]

    @pl.when(pl.program_id(0) == 0)
    def _():
        s_ref[...] = jnp.zeros_like(s_ref)

    for d, o_ref in ((0, of_ref), (1, ob_ref)):
        u_ref, w_ref, qd_ref, kd_ref, qk_ref, gtot_ref = ins[6 * d:6 * d + 6]
        for bi in range(u_ref.shape[1]):
            last = jnp.exp(gtot_ref[bi, 0:1, :])
            for h in range(DN_HEADS):
                s = s_ref[d, bi, h]
                wq = jnp.concatenate([w_ref[0, bi, h], qd_ref[0, bi, h]], axis=0)
                r = _dot(wq, s.astype(BF16))
                v_new = (u_ref[0, bi, h] - r[:CHUNK]).astype(BF16)
                o_ref[bi, :, h * HEAD_DIM:(h + 1) * HEAD_DIM] = r[CHUNK:] + _dot(qk_ref[0, bi, h], v_new)
                j = d * DN_HEADS + h
                s_ref[d, bi, h] = s * last[:, j:j + 1] + _dot_tn(kd_ref[0, bi, h], v_new)


def _dn_scan(u, w, qd, kd, qk, gtot, n_ctx):
    _, b, nh, t, _ = u.shape
    fwd, bwd, nc = _chunk_maps(n_ctx, t)

    def specs(d, order):
        hs = lambda width: pl.BlockSpec((1, b, nh, CHUNK, width), lambda i: (d, 0, 0, order(i), 0))
        return [hs(HEAD_DIM)] * 4 + [hs(CHUNK), pl.BlockSpec((b, CHUNK, LANES), lambda i: (0, order(i), 0))]

    args = (u, w, qd, kd, qk, gtot)
    return pl.pallas_call(
        _dn_scan_kernel,
        grid=(nc,),
        in_specs=specs(0, fwd) + specs(1, bwd),
        out_specs=[
            pl.BlockSpec((b, CHUNK, DN_WIDTH), lambda i: (0, fwd(i), 0)),
            pl.BlockSpec((b, CHUNK, DN_WIDTH), lambda i: (0, bwd(i), 0)),
        ],
        out_shape=[jax.ShapeDtypeStruct((b, t, DN_WIDTH), F32)] * 2,
        scratch_shapes=[pltpu.VMEM((2, b, nh, HEAD_DIM, HEAD_DIM), F32)],
        compiler_params=_cparams(("arbitrary",)),
        name="dn_scan",
    )(*args, *args)


def _mix_finish_kernel(of_ref, ob_ref, z_ref, g_ref, o_ref):
    g = g_ref[...]
    for h in range(of_ref.shape[2] // HEAD_DIM):
        sl = slice(h * HEAD_DIM, (h + 1) * HEAD_DIM)
        o = of_ref[0, :, sl] + ob_ref[0, :, sl]
        y = o * lax.rsqrt(jnp.mean(o * o, axis=-1, keepdims=True) + EPS) * g
        o_ref[0, :, sl] = (y * _silu(z_ref[0, :, sl])).astype(o_ref.dtype)


def _mix_finish(o_f, o_b, p, gate_col, gain):
    b, t, width = o_f.shape
    tt = _pick(t, 256, SUBLANES)
    spec = pl.BlockSpec((1, tt, width), lambda bi, j: (bi, j, 0))
    return pl.pallas_call(
        _mix_finish_kernel,
        grid=(b, t // tt),
        in_specs=[spec, spec,
                  pl.BlockSpec((1, tt, width), lambda bi, j: (bi, j, gate_col // width)),
                  pl.BlockSpec((1, HEAD_DIM), lambda bi, j: (0, 0))],
        out_specs=spec,
        out_shape=jax.ShapeDtypeStruct((b, t, width), BF16),
        compiler_params=_cparams(("parallel", "parallel")),
        name="mix_finish",
    )(o_f, o_b, p, gain.astype(F32).reshape(1, HEAD_DIM))


def _out_proj_kernel(a1_ref, a2_ref, a3_ref, w1_ref, w2_ref, w3_ref, x_ref, g_ref, o_ref):
    y = _dot(a1_ref[0], w1_ref[...]) + _dot(a2_ref[0], w2_ref[...]) + _dot(a3_ref[0], w3_ref[...])
    o_ref[0] = x_ref[0] + g_ref[0] * y


def _out_proj(oa, ob, og, w_out, x, mod_rows, layer, n_ctx):
    b, t, d = x.shape
    tm = _pick(n_ctx, 256, SUBLANES)
    tn = _pick(d, 1024, LANES)
    nct = n_ctx // tm
    w = w_out.astype(BF16)
    return pl.pallas_call(
        _out_proj_kernel,
        grid=(d // tn, b, t // tm),
        in_specs=[
            pl.BlockSpec((1, tm, DN_WIDTH), lambda n, bi, i: (bi, i, 0)),
            pl.BlockSpec((1, tm, HG_WIDTH), lambda n, bi, i: (bi, i, 0)),
            pl.BlockSpec((1, tm, ATT_Q_WIDTH), lambda n, bi, i: (bi, i, 0)),
            pl.BlockSpec((DN_WIDTH, tn), lambda n, bi, i: (0, n)),
            pl.BlockSpec((HG_WIDTH, tn), lambda n, bi, i: (1, n)),
            pl.BlockSpec((ATT_Q_WIDTH, tn), lambda n, bi, i: (1, n)),
            pl.BlockSpec((1, tm, tn), lambda n, bi, i: (bi, i, n)),
            pl.BlockSpec((1, 1, tn), lambda n, bi, i: (
                (layer * SUBLANES + jnp.where(i < nct, b, bi)) * N_MOD + 2, 0, n)),
        ],
        out_specs=pl.BlockSpec((1, tm, tn), lambda n, bi, i: (bi, i, n)),
        out_shape=jax.ShapeDtypeStruct((b, t, d), F32),
        compiler_params=_cparams(("parallel", "parallel", "parallel")),
        name="out_proj",
    )(oa, ob, og, w, w, w, x, mod_rows)


def _router_kernel(h_ref, w_ref, b_ref, idx_ref, wt_ref, rank_ref, cnt_ref, carry_ref):
    i = pl.program_id(0)

    @pl.when(i == 0)
    def _():
        carry_ref[...] = jnp.zeros_like(carry_ref)

    tm = h_ref.shape[0]
    neg = -jnp.inf
    logits = _dot(h_ref[...].astype(BF16), w_ref[...])
    scores = _sigmoid(logits)
    lane = lax.broadcasted_iota(jnp.int32, scores.shape, 1)
    valid = lane < N_EXPERTS
    grp = lane // GROUP_SIZE
    big = jnp.int32(1 << 20)
    sel = jnp.where(valid, scores + b_ref[...], neg)
    rmax = lambda a: jnp.max(a, axis=1, keepdims=True)
    rmin = lambda a: jnp.min(a, axis=1, keepdims=True)

    grp_score = jnp.full(scores.shape, neg, F32)
    for g in range(N_GROUPS):
        in_g = grp == g
        cur = jnp.where(in_g, sel, neg)
        m1 = rmax(cur)
        i1 = rmin(jnp.where(cur == m1, lane, big))
        m2 = rmax(jnp.where(lane == i1, neg, cur))
        grp_score = jnp.where(in_g, m1 + m2, grp_score)
    keep = jnp.zeros(scores.shape, jnp.bool_)
    cur = grp_score
    for _ in range(TOPK_GROUPS):
        m = rmax(cur)
        gsel = rmin(jnp.where(cur == m, grp, big))
        hit = grp == gsel
        keep = jnp.logical_or(keep, hit)
        cur = jnp.where(hit, neg, cur)
    cur = jnp.where(keep, sel, neg)
    idx_out = jnp.zeros(scores.shape, jnp.int32)
    w_out = jnp.zeros(scores.shape, F32)
    onehots = []
    for k in range(TOP_K):
        m = rmax(cur)
        ik = rmin(jnp.where(cur == m, lane, big))
        hit = lane == ik
        sk = jnp.sum(jnp.where(hit, scores, 0.0), axis=1, keepdims=True)
        idx_out = jnp.where(lane == k, ik, idx_out)
        w_out = jnp.where(lane == k, sk, w_out)
        onehots.append(hit)
        cur = jnp.where(hit, neg, cur)
    w_out = w_out * (ROUTED_SCALE / jnp.sum(w_out, axis=1, keepdims=True))
    chosen = onehots[0]
    for oh in onehots[1:]:
        chosen = jnp.logical_or(chosen, oh)
    e_mat = jnp.where(chosen, 1.0, 0.0)
    r_i = lax.broadcasted_iota(jnp.int32, (tm, tm), 0)
    c_i = lax.broadcasted_iota(jnp.int32, (tm, tm), 1)
    lower = jnp.where(c_i < r_i, 1.0, 0.0).astype(BF16)
    before = _dot(lower, e_mat.astype(BF16)) + carry_ref[...]
    rank_out = jnp.zeros(scores.shape, jnp.int32)
    for k in range(TOP_K):
        rk = jnp.sum(jnp.where(onehots[k], before, 0.0), axis=1, keepdims=True)
        rank_out = jnp.where(lane == k, rk.astype(jnp.int32), rank_out)
    carry_ref[...] = carry_ref[...] + jnp.sum(e_mat, axis=0, keepdims=True)
    idx_ref[...] = idx_out
    wt_ref[...] = w_out
    rank_ref[...] = rank_out
    cnt_ref[...] = jnp.broadcast_to(carry_ref[...], cnt_ref.shape).astype(jnp.int32)


def _router(h, w_router, bias):
    n, d = h.shape
    tm = _pick(n, 256, SUBLANES)
    w = jnp.pad(w_router, ((0, 0), (0, LANES - N_EXPERTS))).astype(BF16)
    bz = jnp.pad(bias.astype(F32).reshape(1, -1), ((0, 0), (0, LANES - N_EXPERTS)))
    tile = pl.BlockSpec((tm, LANES), lambda i: (i, 0))
    return pl.pallas_call(
        _router_kernel,
        grid=(n // tm,),
        in_specs=[
            pl.BlockSpec((tm, d), lambda i: (i, 0)),
            pl.BlockSpec((d, LANES), lambda i: (0, 0)),
            pl.BlockSpec((1, LANES), lambda i: (0, 0)),
        ],
        out_specs=[tile, tile, tile, pl.BlockSpec((SUBLANES, LANES), lambda i: (0, 0))],
        out_shape=[
            jax.ShapeDtypeStruct((n, LANES), jnp.int32),
            jax.ShapeDtypeStruct((n, LANES), F32),
            jax.ShapeDtypeStruct((n, LANES), jnp.int32),
            jax.ShapeDtypeStruct((SUBLANES, LANES), jnp.int32),
        ],
        scratch_shapes=[pltpu.VMEM((1, LANES), F32)],
        compiler_params=_cparams(("arbitrary",)),
        name="router",
    )(h, w, bz)


def _pack_halves(y):
    half = y.shape[1] // 2
    bits = lambda v: lax.bitcast_convert_type(v.astype(BF16).astype(F32), jnp.uint32)
    return (bits(y[:, half:]) & jnp.uint32(0xFFFF0000)) | (bits(y[:, :half]) >> 16)


def _unpack_halves(w):
    lo = lax.bitcast_convert_type(w << 16, F32)
    hi = lax.bitcast_convert_type(w & jnp.uint32(0xFFFF0000), F32)
    return lo, hi


def _expert_kernel(be_ref, nu_ref, tok_ref, nx1_ref, nx2_ref, h_hbm, wg_ref, wu_ref, wd_ref, y_ref,
                   xbuf, wgb, wub, wdb, sem, *, n_steps):
    i = pl.program_id(0)
    n_used = nu_ref[0]
    slot = i % EXPERT_BUFS
    ahead = (i + 2) % EXPERT_BUFS

    def row_copy(tok, slot_, r):
        return pltpu.make_async_copy(h_hbm.at[pl.ds(tok, 1)], xbuf.at[slot_, pl.ds(r, 1)], sem.at[slot_])

    def start_gather(idx_ref, slot_):
        for r in range(MOE_ROWS):
            row_copy(idx_ref[0, 0, r], slot_, r).start()

    def wait_gather(slot_):
        pltpu.make_async_copy(h_hbm.at[pl.ds(0, MOE_ROWS)], xbuf.at[slot_], sem.at[slot_]).wait()

    @pl.when(i == 0)
    def _():
        start_gather(tok_ref, 0)
        start_gather(nx1_ref, 1)

    @pl.when(i <= n_used + 1)
    def _():
        wait_gather(slot)

    @pl.when(i < n_used)
    def _():
        @pl.when(jnp.logical_or(i == 0, be_ref[i] != be_ref[jnp.maximum(i - 1, 0)]))
        def _():
            wgb[...] = wg_ref[0, 0].astype(BF16)
            wub[...] = wu_ref[0, 0].astype(BF16)
            wdb[...] = wd_ref[0, 0].astype(BF16)

        x = xbuf[slot].astype(BF16)
        start_gather(nx2_ref, ahead)
        a = _silu(_dot(x, wgb[...])) * _dot(x, wub[...])
        y_ref[...] = _pack_halves(_dot(a.astype(BF16), wdb[...]))

    @pl.when(i >= n_used)
    def _():
        y_ref[...] = jnp.zeros_like(y_ref)

    @pl.when(i == n_steps - 1)
    def _():
        for j in (n_steps, n_steps + 1):
            @pl.when(j - 2 < n_used)
            def _():
                wait_gather(j % EXPERT_BUFS)


def _routed_experts(h, row_tok, block_e, n_used, wg, wu, wd, layer):
    n, d = h.shape
    n_blocks = block_e.shape[0]
    ff = wg.shape[3]
    assert n_blocks >= 2
    tok3 = row_tok.reshape(n_blocks, 1, MOE_ROWS)
    idx_spec = lambda k: pl.BlockSpec((1, 1, MOE_ROWS), lambda i, be, nu: (jnp.minimum(i + k, n_blocks - 1), 0, 0),
                                      memory_space=pltpu.SMEM)
    w_spec = lambda shape: pl.BlockSpec((1, 1) + shape, lambda i, be, nu: (layer, be[i], 0, 0))
    grid_spec = pltpu.PrefetchScalarGridSpec(
        num_scalar_prefetch=2,
        grid=(n_blocks,),
        in_specs=[
            idx_spec(0), idx_spec(1), idx_spec(2),
            pl.BlockSpec(memory_space=pl.ANY),
            w_spec((d, ff)), w_spec((d, ff)), w_spec((ff, d)),
        ],
        out_specs=pl.BlockSpec((MOE_ROWS, d // 2), lambda i, be, nu: (i, 0)),
        scratch_shapes=[pltpu.VMEM((EXPERT_BUFS, MOE_ROWS, d), F32),
                        pltpu.VMEM((d, ff), BF16), pltpu.VMEM((d, ff), BF16), pltpu.VMEM((ff, d), BF16),
                        pltpu.SemaphoreType.DMA((EXPERT_BUFS,))],
    )
    return pl.pallas_call(
        functools.partial(_expert_kernel, n_steps=n_blocks),
        grid_spec=grid_spec,
        out_shape=jax.ShapeDtypeStruct((n_blocks * MOE_ROWS, d // 2), jnp.uint32),
        compiler_params=_cparams(("arbitrary",)),
        name="routed_experts",
    )(block_e, n_used, tok3, tok3, tok3, h, wg, wu, wd)


def _shared_expert_kernel(h_ref, wsg_ref, wsu_ref, wsd_ref, o_ref):
    hb = h_ref[...].astype(BF16)
    a = _silu(_dot(hb, wsg_ref[...])) * _dot(hb, wsu_ref[...])
    o_ref[...] = _dot(a.astype(BF16), wsd_ref[...])


def _shared_expert(h, wsg, wsu, wsd):
    n, d = h.shape
    ff = wsg.shape[1]
    tm = _pick(n, 512, SUBLANES)
    tile = pl.BlockSpec((tm, d), lambda i: (i, 0))
    return pl.pallas_call(
        _shared_expert_kernel,
        grid=(n // tm,),
        in_specs=[tile,
                  pl.BlockSpec((d, ff), lambda i: (0, 0)),
                  pl.BlockSpec((d, ff), lambda i: (0, 0)),
                  pl.BlockSpec((ff, d), lambda i: (0, 0))],
        out_specs=tile,
        out_shape=jax.ShapeDtypeStruct((n, d), F32),
        compiler_params=_cparams(("parallel",)),
        name="shared_expert",
    )(h, wsg, wsu, wsd)


def _combine_kernel(pos_ref, nxt_ref, y_hbm, wt_ref, sh_ref, x_ref, g_ref, o_ref, ybuf, sem, *, tt):
    i = pl.program_id(0)
    n = pl.num_programs(0)
    slot = i % 2
    rows = tt * TOP_K

    def row_copy(src, slot_, r):
        return pltpu.make_async_copy(y_hbm.at[pl.ds(src, 1)], ybuf.at[slot_, pl.ds(r, 1)], sem.at[slot_])

    def start_gather(idx_ref, slot_):
        for r in range(rows):
            row_copy(idx_ref[0, 0, r], slot_, r).start()

    def wait_gather(slot_):
        pltpu.make_async_copy(y_hbm.at[pl.ds(0, rows)], ybuf.at[slot_], sem.at[slot_]).wait()

    @pl.when(i == 0)
    def _():
        start_gather(pos_ref, 0)

    wait_gather(slot)
    start_gather(nxt_ref, 1 - slot)
    wt = wt_ref[...]
    half = x_ref.shape[1] // 2
    cw = min(COMBINE_COLS, half)
    for c0 in range(0, half, cw):
        cl = slice(c0, c0 + cw)
        ch = slice(half + c0, half + c0 + cw)
        acc_lo = sh_ref[:, cl]
        acc_hi = sh_ref[:, ch]
        for k in range(TOP_K):
            lo, hi = _unpack_halves(ybuf[slot, k * tt:(k + 1) * tt, cl])
            acc_lo = acc_lo + wt[:, k:k + 1] * lo
            acc_hi = acc_hi + wt[:, k:k + 1] * hi
        o_ref[:, cl] = x_ref[:, cl] + g_ref[0, :, cl] * acc_lo
        o_ref[:, ch] = x_ref[:, ch] + g_ref[0, :, ch] * acc_hi

    @pl.when(i == n - 1)
    def _():
        wait_gather(1 - slot)


def _combine(y, pos, wts, shared, x, mod_rows, layer, batch, n_ctx):
    n, d = x.shape
    t = n // batch
    tt = _pick(n_ctx, 64, SUBLANES)
    n_tiles = n // tt
    tiles_per_b = t // tt
    nct = n_ctx // tt
    pos3 = pos.reshape(n_tiles, tt, TOP_K).transpose(0, 2, 1).reshape(n_tiles, 1, tt * TOP_K)
    row = lambda i: (layer * SUBLANES + jnp.where(i % tiles_per_b < nct, batch, i // tiles_per_b)) * N_MOD + 5
    kern = functools.partial(_combine_kernel, tt=tt)
    tile = pl.BlockSpec((tt, d), lambda i: (i, 0))
    return pl.pallas_call(
        kern,
        grid=(n_tiles,),
        in_specs=[
            pl.BlockSpec((1, 1, tt * TOP_K), lambda i: (i, 0, 0), memory_space=pltpu.SMEM),
            pl.BlockSpec((1, 1, tt * TOP_K), lambda i: (jnp.minimum(i + 1, n_tiles - 1), 0, 0),
                         memory_space=pltpu.SMEM),
            pl.BlockSpec(memory_space=pl.ANY),
            pl.BlockSpec((tt, LANES), lambda i: (i, 0)),
            tile, tile,
            pl.BlockSpec((1, 1, d), lambda i: (row(i), 0, 0)),
        ],
        out_specs=tile,
        out_shape=jax.ShapeDtypeStruct((n, d), F32),
        scratch_shapes=[pltpu.VMEM((2, tt * TOP_K, d // 2), jnp.uint32), pltpu.SemaphoreType.DMA((2,))],
        compiler_params=_cparams(("arbitrary",)),
        name="moe_combine",
    )(pos3, pos3, y, wts, shared, x, mod_rows)


def _moe(h, x, mod_rows, layer, w_router, router_bias, wg, wu, wd, wsg, wsu, wsd, batch, n_ctx):
    n, d = h.shape
    idx, wts, rank, counts = _router(h, w_router, router_bias)
    idx = idx[:, :TOP_K]
    rank = rank[:, :TOP_K]
    counts = counts[0, :N_EXPERTS]
    padded = (counts + MOE_ROWS - 1) // MOE_ROWS * MOE_ROWS
    pad_end = jnp.cumsum(padded)
    pad_start = pad_end - padded
    pos = pad_start[idx] + rank
    n_blocks = -(-(n * TOP_K) // MOE_ROWS) + N_EXPERTS
    tok = jnp.broadcast_to(jnp.arange(n, dtype=jnp.int32)[:, None], pos.shape)
    row_tok = jnp.zeros((n_blocks * MOE_ROWS,), jnp.int32).at[pos.reshape(-1)].set(tok.reshape(-1))
    block_start = jnp.arange(n_blocks, dtype=jnp.int32) * MOE_ROWS
    block_e = jnp.minimum(jnp.sum(pad_end[None, :] <= block_start[:, None], axis=1), N_EXPERTS - 1).astype(jnp.int32)
    n_used = (pad_end[-1] // MOE_ROWS).astype(jnp.int32).reshape(1)
    y = _routed_experts(h, row_tok, block_e, n_used, wg, wu, wd, layer)
    shared = _shared_expert(h, wsg, wsu, wsd)
    return _combine(y, pos.astype(jnp.int32), wts, shared, x, mod_rows, layer, batch, n_ctx)


def _rope_tables(n_ctx, n_lat):
    rows = n_lat // GRID_W
    row, col = jnp.meshgrid(jnp.arange(rows), jnp.arange(GRID_W), indexing="ij")
    half = HEAD_DIM // 2
    inv_freq = ROPE_THETA ** (-jnp.arange(0, half, 2, dtype=F32) / half)
    ang_r = row.reshape(-1, 1).astype(F32) * inv_freq
    ang_c = col.reshape(-1, 1).astype(F32) * inv_freq
    ang = jnp.concatenate([ang_r, ang_r, ang_c, ang_c], axis=-1)
    cos = jnp.concatenate([jnp.ones((n_ctx, HEAD_DIM), F32), jnp.cos(ang)], axis=0)
    sin = jnp.concatenate([jnp.zeros((n_ctx, HEAD_DIM), F32), jnp.sin(ang)], axis=0)
    first = (jnp.arange(HEAD_DIM) % 64) < 32
    return cos, jnp.where(first[None, :], -sin, sin)


def kernel(x, c, ctx, c_ctx, w_mod, b_mod, norm1_g, norm2_g, w_in, w_out, dn_conv_w, dn_a_log, dn_dt_bias, dn_norm_g, hg_lb_logits, hg_norm_g, att_q_norm_g, att_k_norm_g, w_router, router_bias, w_exp_gate, w_exp_up, w_exp_down, w_sh_gate, w_sh_up, w_sh_down, final_norm_g):
    bsz, n_lat, d = x.shape
    n_ctx = ctx.shape[1]
    depth = w_in.shape[0]
    t = n_ctx + n_lat
    assert bsz + 1 <= SUBLANES and n_ctx % CHUNK == 0 and n_lat % CHUNK == 0 and n_lat % GRID_W == 0

    cos, sin_s = _rope_tables(n_ctx, n_lat)
    lb_p = jax.nn.softmax(hg_lb_logits.astype(F32), axis=0)
    lower_bounds = jnp.cumsum(lb_p, axis=0) - lb_p[0]

    c_all = jnp.concatenate([c, c_ctx[None, :], jnp.zeros((SUBLANES - bsz - 1, d), c.dtype)], axis=0)
    mod = _mod_vectors(c_all, w_mod, b_mod)
    mod_rows = mod.reshape(depth * SUBLANES * N_MOD, 1, d)

    xs = jnp.concatenate([ctx, x], axis=1)
    for l in range(depth):
        h1 = _norm_mod(xs, norm1_g, mod_rows, l, 0, n_ctx, BF16)
        p = _matmul(h1.reshape(bsz * t, d), _permute_w_in(w_in[l]), F32).reshape(bsz, t, P_WIDTH)
        qr, kr, vr = _att_prep(p, cos, sin_s, att_q_norm_g[l], att_k_norm_g[l])
        og = _attention(qr, kr, vr, n_ctx)
        hf, hb = _hgrn2(p, lower_bounds[l], n_ctx)
        ob = _mix_finish(hf, hb, p, C_HG, hg_norm_g[l])
        dq, dk, dv, gcol, gtot, gcr = _dn_prep(p, dn_conv_w[l], dn_a_log[l], dn_dt_bias[l], n_ctx)
        u, w, qd, kd, qk = _dn_intra(dq, dk, dv, gcol, gtot, gcr)
        df, db = _dn_scan(u, w, qd, kd, qk, gtot, n_ctx)
        oa = _mix_finish(df, db, p, C_DZ, dn_norm_g[l])
        xs = _out_proj(oa, ob, og, w_out[l], xs, mod_rows, l, n_ctx)
        h2 = _norm_mod(xs, norm2_g, mod_rows, l, 3, n_ctx, F32).reshape(bsz * t, d)
        xs = _moe(h2, xs.reshape(bsz * t, d), mod_rows, l, w_router[l], router_bias[l],
                  w_exp_gate, w_exp_up, w_exp_down,
                  w_sh_gate[l].astype(BF16), w_sh_up[l].astype(BF16), w_sh_down[l].astype(BF16),
                  bsz, n_ctx).reshape(bsz, t, d)
    return _final_norm(xs, final_norm_g, n_ctx)
```
